```python
import math
import jax, jax.numpy as jnp
from jax import lax
import numpy as np

D_MODEL = 1024
BATCH = 8
SEQ = 2048
DEPTH = 2
DEC_BATCH = 128
DEC_SEQ = 1
PAST_LEN = 16384
PAGE_SIZE = 128

N_MIXERS = 2
N_SSD_LAYERS = (DEPTH + 1) // 2
N_MLSTM_LAYERS = DEPTH // 2
EXPAND = 2
D_INNER = EXPAND * D_MODEL
CONV_WIDTH = 4
SSD_HEAD_DIM = 64
SSD_HEADS = D_INNER // SSD_HEAD_DIM
SSD_GROUPS = 4
SSD_HPG = SSD_HEADS // SSD_GROUPS
SSD_STATE = 128
SSD_GN = SSD_GROUPS * SSD_STATE
SSD_CONV_DIM = D_INNER + 2 * SSD_GN
SSD_IN_DIM = D_INNER + SSD_CONV_DIM + SSD_HEADS
SSD_CHUNK = 128
MLSTM_HEADS = 4
MLSTM_HEAD_DIM = D_INNER // MLSTM_HEADS
MLSTM_IN_DIM = 3 * D_INNER
MLSTM_CHUNK = 64
EPS = 1e-6
F32 = jnp.float32

kernel_name = "hybrid_ssd_mlstm_sandwich_decode_step"


def rms_norm(x, w):
    xf = x.astype(F32)
    y = xf * lax.rsqrt(jnp.mean(xf * xf, axis=-1, keepdims=True) + EPS)
    return (y * w.astype(F32)).astype(x.dtype)


def causal_conv(xs, buf, w, b):
    L = xs.shape[1]
    xp = jnp.concatenate([buf.astype(F32), xs], axis=1)
    wf = w.astype(F32)
    y = b.astype(F32)
    for k in range(CONV_WIDTH):
        y = y + xp[:, k:k + L] * wf[k]
    return y, xp[:, L:]


def _pad_time(a, lp, value=0.0):
    pad = lp - a.shape[1]
    if pad == 0:
        return a
    widths = [(0, 0)] * a.ndim
    widths[1] = (0, pad)
    return jnp.pad(a, widths, constant_values=value)


def _chunks(a, q):
    b, lp = a.shape[:2]
    return jnp.moveaxis(a.reshape((b, lp // q, q) + a.shape[2:]), 1, 0)


def _unchunk(a, L):
    nc, b, q = a.shape[:3]
    return jnp.moveaxis(a, 0, 1).reshape((b, nc * q) + a.shape[3:])[:, :L]


def ssd_scan(x, dt, a, bm, cm, h0):
    L = x.shape[1]
    q = min(SSD_CHUNK, L)
    lp = -(-L // q) * q
    xs = tuple(_chunks(_pad_time(t, lp), q) for t in (x, dt, bm, cm))
    causal = jnp.tril(jnp.ones((q, q), dtype=bool))

    def step(h, inp):
        xc, dtc, bc, cc = inp
        cum = jnp.cumsum(dtc * a, axis=1)
        seg = cum[:, :, None] - cum[:, None, :]
        decay = jnp.exp(jnp.where(causal[None, :, :, None, None], seg, -jnp.inf))
        xdt = xc * dtc[..., None]
        cb = jnp.einsum('blgn,bsgn->blsg', cc, bc)
        y = (jnp.einsum('blsg,blsgr,bsgrp->blgrp', cb, decay, xdt)
             + jnp.einsum('blgn,bgrpn,blgr->blgrp', cc, h, jnp.exp(cum)))
        tail = jnp.exp(cum[:, -1:] - cum)
        h_new = (h * jnp.exp(cum[:, -1])[..., None, None]
                 + jnp.einsum('bsgn,bsgr,bsgrp->bgrpn', bc, tail, xdt))
        return h_new, y

    h_t, ys = lax.scan(step, h0, xs)
    return _unchunk(ys, L), h_t


def mlstm_scan(q, k, v, log_i, log_f, c0, n0, m0):
    L = q.shape[1]
    qlen = min(MLSTM_CHUNK, L)
    lp = -(-L // qlen) * qlen
    xs = (_chunks(_pad_time(q, lp), qlen), _chunks(_pad_time(k, lp), qlen),
          _chunks(_pad_time(v, lp), qlen), _chunks(_pad_time(log_i, lp, -jnp.inf), qlen),
          _chunks(_pad_time(log_f, lp), qlen))
    causal = jnp.tril(jnp.ones((qlen, qlen), dtype=bool))

    def step(carry, inp):
        c, n, m = carry
        qc, kc, vc, lic, lfc = inp
        bcum = jnp.cumsum(lfc, axis=1)
        dmat = bcum[:, :, None] - bcum[:, None, :] + lic[:, None, :]
        dmat = jnp.where(causal[None, :, :, None], dmat, -jnp.inf)
        inter = bcum + m[:, None]
        m_t = jnp.maximum(inter, jnp.max(dmat, axis=2))
        w = jnp.exp(dmat - m_t[:, :, None])
        s = jnp.einsum('blhd,bshd->blsh', qc, kc) * w
        inter_w = jnp.exp(inter - m_t)
        num = (jnp.einsum('blsh,bshe->blhe', s, vc)
               + inter_w[..., None] * jnp.einsum('blhd,bhde->blhe', qc, c))
        den = jnp.sum(s, axis=2) + inter_w * jnp.einsum('blhd,bhd->blh', qc, n)
        h = num / jnp.maximum(jnp.abs(den), jnp.exp(-m_t))[..., None]
        m_new = m_t[:, -1]
        wts = jnp.exp(bcum[:, -1:] - bcum + lic - m_new[:, None])
        carry_decay = jnp.exp(bcum[:, -1] + m - m_new)
        c_new = carry_decay[..., None, None] * c + jnp.einsum('bsh,bshd,bshe->bhde', wts, kc, vc)
        n_new = carry_decay[..., None] * n + jnp.einsum('bsh,bshd->bhd', wts, kc)
        return (c_new, n_new, m_new), h

    (c_t, n_t, m_t), hs = lax.scan(step, (c0, n0, m0), xs)
    return _unchunk(hs, L), c_t, n_t, m_t


def ssd_mixer(u, conv_buf, h0, w_in, conv_w, conv_b, dt_bias, a_log, d_skip, norm_w, w_out):
    bsz, L, _ = u.shape
    proj = jnp.matmul(u, w_in).astype(F32)
    z = proj[..., :D_INNER]
    xbc = proj[..., D_INNER:D_INNER + SSD_CONV_DIM]
    dt_pre = proj[..., D_INNER + SSD_CONV_DIM:]
    xbc, new_buf = causal_conv(xbc, conv_buf, conv_w, conv_b)
    xbc = jax.nn.silu(xbc)
    x = xbc[..., :D_INNER].reshape(bsz, L, SSD_GROUPS, SSD_HPG, SSD_HEAD_DIM)
    bm = xbc[..., D_INNER:D_INNER + SSD_GN].reshape(bsz, L, SSD_GROUPS, SSD_STATE)
    cm = xbc[..., D_INNER + SSD_GN:].reshape(bsz, L, SSD_GROUPS, SSD_STATE)
    dt = jax.nn.softplus(dt_pre + dt_bias.astype(F32)).reshape(bsz, L, SSD_GROUPS, SSD_HPG)
    a = -jnp.exp(a_log.astype(F32)).reshape(SSD_GROUPS, SSD_HPG)
    h0 = h0.astype(F32).reshape(bsz, SSD_GROUPS, SSD_HPG, SSD_HEAD_DIM, SSD_STATE)
    y, h_t = ssd_scan(x, dt, a, bm, cm, h0)
    y = y + d_skip.astype(F32).reshape(SSD_GROUPS, SSD_HPG, 1) * x
    y = y.reshape(bsz, L, D_INNER) * jax.nn.silu(z)
    yg = y.reshape(bsz, L, SSD_GROUPS, D_INNER // SSD_GROUPS)
    yg = yg * lax.rsqrt(jnp.mean(yg * yg, axis=-1, keepdims=True) + EPS)
    y = yg.reshape(bsz, L, D_INNER) * norm_w.astype(F32)
    out = jnp.matmul(y.astype(u.dtype), w_out)
    return out, new_buf, h_t.reshape(bsz, SSD_HEADS, SSD_HEAD_DIM, SSD_STATE)


def mlstm_mixer(u, conv_buf, c0, n0, m0, w_in, conv_w, conv_b, w_q, w_k, w_v, w_gate, b_gate,
                head_norm_w, skip, w_out):
    bsz, L, _ = u.shape
    proj = jnp.matmul(u, w_in).astype(F32)
    xm = proj[..., :D_INNER]
    z = proj[..., D_INNER:2 * D_INNER]
    o_pre = proj[..., 2 * D_INNER:]
    xc, new_buf = causal_conv(xm, conv_buf, conv_w, conv_b)
    xc = jax.nn.silu(xc)
    xch = xc.reshape(bsz, L, MLSTM_HEADS, MLSTM_HEAD_DIM)
    xmh = xm.reshape(bsz, L, MLSTM_HEADS, MLSTM_HEAD_DIM)
    q = jnp.einsum('blhd,hde->blhe', xch, w_q.astype(F32))
    k = jnp.einsum('blhd,hde->blhe', xch, w_k.astype(F32))
    v = jnp.einsum('blhd,hde->blhe', xmh, w_v.astype(F32))
    qkv = jnp.concatenate([q.reshape(bsz, L, D_INNER), k.reshape(bsz, L, D_INNER),
                           v.reshape(bsz, L, D_INNER)], axis=-1)
    gates = jnp.matmul(qkv, w_gate.astype(F32)) + b_gate.astype(F32)
    log_i = gates[..., :MLSTM_HEADS]
    log_f = jax.nn.log_sigmoid(gates[..., MLSTM_HEADS:])
    q = q * (MLSTM_HEAD_DIM ** -0.5)
    h, c_t, n_t, m_t = mlstm_scan(q, k, v, log_i, log_f, c0.astype(F32), n0.astype(F32),
                                  m0.astype(F32))
    mu = jnp.mean(h, axis=-1, keepdims=True)
    hc = h - mu
    h = hc * lax.rsqrt(jnp.mean(hc * hc, axis=-1, keepdims=True) + EPS)
    h = h.reshape(bsz, L, D_INNER) * head_norm_w.astype(F32)
    h = jax.nn.sigmoid(o_pre) * h + skip.astype(F32) * xc
    h = h * jax.nn.silu(z)
    out = jnp.matmul(h.astype(u.dtype), w_out)
    return out, new_buf, c_t, n_t, m_t


def run_trunk(x, states, ssd_p, ml_p, pre_norm_w, post_norm_w):
    ssd_conv, ssd_h, ml_conv, ml_c, ml_n, ml_m = states
    o_sc, o_sh, o_mc, o_c, o_n, o_m = [], [], [], [], [], []
    for i in range(DEPTH):
        j = i // N_MIXERS
        u = rms_norm(x, pre_norm_w[i])
        if i % N_MIXERS == 0:
            out, cb, hs = ssd_mixer(u, ssd_conv[j], ssd_h[j], *[p[j] for p in ssd_p])
            o_sc.append(cb.astype(ssd_conv.dtype))
            o_sh.append(hs.astype(ssd_h.dtype))
        else:
            out, cb, c, n, m = mlstm_mixer(u, ml_conv[j], ml_c[j], ml_n[j], ml_m[j],
                                           *[p[j] for p in ml_p])
            o_mc.append(cb.astype(ml_conv.dtype))
            o_c.append(c.astype(ml_c.dtype))
            o_n.append(n.astype(ml_n.dtype))
            o_m.append(m.astype(ml_m.dtype))
        x = x + rms_norm(out, post_norm_w[i])
    return x, (jnp.stack(o_sc), jnp.stack(o_sh), jnp.stack(o_mc), jnp.stack(o_c),
               jnp.stack(o_n), jnp.stack(o_m))


def setup_inputs(seed: int = 0) -> dict:
    key = jax.random.key(seed)
    ks = jax.random.split(key, 32)
    nrm = jax.random.normal
    ns, nm = N_SSD_LAYERS, N_MLSTM_LAYERS
    dt0 = jnp.exp(jax.random.uniform(ks[14], (ns, SSD_HEADS), minval=math.log(1e-3),
                                     maxval=math.log(1e-1)))
    f_bias = jnp.linspace(3.0, 6.0, MLSTM_HEADS)[None] + 0.05 * nrm(ks[26], (nm, MLSTM_HEADS))
    i_bias = 0.1 * nrm(ks[27], (nm, MLSTM_HEADS))
    return {
        "x_prompt": nrm(ks[0], (BATCH, SEQ, D_MODEL), F32),
        "x_sample": nrm(ks[1], (DEC_BATCH, DEC_SEQ, D_MODEL), F32),
        "state_ssd_conv": nrm(ks[2], (ns, DEC_BATCH, CONV_WIDTH - 1, SSD_CONV_DIM), F32),
        "state_ssd": 0.5 * nrm(ks[3], (ns, DEC_BATCH, SSD_HEADS, SSD_HEAD_DIM, SSD_STATE), F32),
        "state_mlstm_conv": nrm(ks[4], (nm, DEC_BATCH, CONV_WIDTH - 1, D_INNER), F32),
        "state_mlstm_c": 0.1 * nrm(ks[5], (nm, DEC_BATCH, MLSTM_HEADS, MLSTM_HEAD_DIM, MLSTM_HEAD_DIM), F32),
        "state_mlstm_n": 0.1 * nrm(ks[6], (nm, DEC_BATCH, MLSTM_HEADS, MLSTM_HEAD_DIM), F32),
        "state_mlstm_m": nrm(ks[7], (nm, DEC_BATCH, MLSTM_HEADS), F32),
        "pre_norm_w": 1.0 + 0.02 * nrm(ks[8], (DEPTH, D_MODEL), F32),
        "post_norm_w": 1.0 + 0.02 * nrm(ks[9], (DEPTH, D_MODEL), F32),
        "ssd_w_in": nrm(ks[10], (ns, D_MODEL, SSD_IN_DIM), F32) * D_MODEL ** -0.5,
        "ssd_conv_w": nrm(ks[11], (ns, CONV_WIDTH, SSD_CONV_DIM), F32) * CONV_WIDTH ** -0.5,
        "ssd_conv_b": 0.02 * nrm(ks[12], (ns, SSD_CONV_DIM), F32),
        "ssd_dt_bias": dt0 + jnp.log(-jnp.expm1(-dt0)),
        "ssd_a_log": jnp.log(jax.random.uniform(ks[15], (ns, SSD_HEADS), minval=1.0, maxval=16.0)),
        "ssd_d": 1.0 + 0.02 * nrm(ks[16], (ns, SSD_HEADS), F32),
        "ssd_norm_w": 1.0 + 0.02 * nrm(ks[17], (ns, D_INNER), F32),
        "ssd_w_out": nrm(ks[18], (ns, D_INNER, D_MODEL), F32) * D_INNER ** -0.5,
        "ml_w_in": nrm(ks[19], (nm, D_MODEL, MLSTM_IN_DIM), F32) * D_MODEL ** -0.5,
        "ml_conv_w": nrm(ks[20], (nm, CONV_WIDTH, D_INNER), F32) * CONV_WIDTH ** -0.5,
        "ml_conv_b": 0.02 * nrm(ks[21], (nm, D_INNER), F32),
        "ml_w_q": nrm(ks[22], (nm, MLSTM_HEADS, MLSTM_HEAD_DIM, MLSTM_HEAD_DIM), F32) * MLSTM_HEAD_DIM ** -0.5,
        "ml_w_k": nrm(ks[23], (nm, MLSTM_HEADS, MLSTM_HEAD_DIM, MLSTM_HEAD_DIM), F32) * MLSTM_HEAD_DIM ** -0.5,
        "ml_w_v": nrm(ks[24], (nm, MLSTM_HEADS, MLSTM_HEAD_DIM, MLSTM_HEAD_DIM), F32) * MLSTM_HEAD_DIM ** -0.5,
        "ml_w_gate": 0.1 * nrm(ks[25], (nm, 3 * D_INNER, 2 * MLSTM_HEADS), F32) * (3 * D_INNER) ** -0.5,
        "ml_b_gate": jnp.concatenate([i_bias, f_bias], axis=-1),
        "ml_head_norm_w": 1.0 + 0.02 * nrm(ks[28], (nm, D_INNER), F32),
        "ml_skip": 1.0 + 0.02 * nrm(ks[29], (nm, D_INNER), F32),
        "ml_w_out": nrm(ks[30], (nm, D_INNER, D_MODEL), F32) * D_INNER ** -0.5,
    }


def reference(x_prompt, x_sample, state_ssd_conv, state_ssd, state_mlstm_conv, state_mlstm_c,
              state_mlstm_n, state_mlstm_m, pre_norm_w, post_norm_w, ssd_w_in, ssd_conv_w,
              ssd_conv_b, ssd_dt_bias, ssd_a_log, ssd_d, ssd_norm_w, ssd_w_out, ml_w_in, ml_conv_w,
              ml_conv_b, ml_w_q, ml_w_k, ml_w_v, ml_w_gate, ml_b_gate, ml_head_norm_w, ml_skip,
              ml_w_out):
    ssd_p = (ssd_w_in, ssd_conv_w, ssd_conv_b, ssd_dt_bias, ssd_a_log, ssd_d, ssd_norm_w, ssd_w_out)
    ml_p = (ml_w_in, ml_conv_w, ml_conv_b, ml_w_q, ml_w_k, ml_w_v, ml_w_gate, ml_b_gate,
            ml_head_norm_w, ml_skip, ml_w_out)
    bp = x_prompt.shape[0]
    zero_states = tuple(jnp.zeros((s.shape[0], bp) + s.shape[2:], s.dtype) for s in
                        (state_ssd_conv, state_ssd, state_mlstm_conv, state_mlstm_c,
                         state_mlstm_n, state_mlstm_m))
    y_prompt, (p_sc, p_sh, p_mc, p_c, p_n, p_m) = run_trunk(
        x_prompt, zero_states, ssd_p, ml_p, pre_norm_w, post_norm_w)
    sample_states = (state_ssd_conv, state_ssd, state_mlstm_conv, state_mlstm_c,
                     state_mlstm_n, state_mlstm_m)
    y_sample, (s_sc, s_sh, s_mc, s_c, s_n, s_m) = run_trunk(
        x_sample, sample_states, ssd_p, ml_p, pre_norm_w, post_norm_w)
    return (y_prompt, y_sample, p_sc, s_sc, p_sh, s_sh, p_mc, s_mc, p_c, s_c, p_n, s_n, p_m, s_m)
```

```python
import functools

import jax
import jax.numpy as jnp
from jax import lax
from jax.experimental import pallas as pl
from jax.experimental.pallas import tpu as pltpu

F32 = jnp.float32
BF16 = jnp.bfloat16

D_MODEL = 1024
D_INNER = 2048
CONV_WIDTH = 4
SSD_HEAD_DIM = 64
SSD_HEADS = 32
SSD_GROUPS = 4
SSD_STATE = 128
SSD_GN = SSD_GROUPS * SSD_STATE
SSD_CONV_DIM = D_INNER + 2 * SSD_GN
SSD_CHUNK = 128
MLSTM_HEADS = 4
MLSTM_HEAD_DIM = 512
EPS = 1e-6

LANES = 128
SUBLANES = 8
HALO = SUBLANES
VMEM_LIMIT = 58 * 1024 * 1024

L0_TB = 256


def _sigmoid(x):
    return 1.0 / (1.0 + jnp.exp(-x))


def _silu(x):
    return x * _sigmoid(x)


def _softplus(x):
    return jnp.maximum(x, 0.0) + jnp.log1p(jnp.exp(-jnp.abs(x)))


def _rms(x, w):
    ms = jnp.mean(x * x, axis=-1, keepdims=True)
    return x * lax.rsqrt(ms + EPS) * w


def _dot(a, b):
    return jnp.dot(a, b, preferred_element_type=F32)


def _dot_exact(a, b):
    return jnp.dot(a, b, preferred_element_type=F32, precision=lax.Precision.HIGHEST)


def _tri(n):
    r = lax.broadcasted_iota(jnp.int32, (n, n), 0)
    c = lax.broadcasted_iota(jnp.int32, (n, n), 1)
    return r >= c


def _const_spec(shape):
    nd = len(shape)
    return pl.BlockSpec(shape, lambda *_: (0,) * nd, pipeline_mode=pl.Buffered(1))


def _ssd_prompt_kernel(x_ref, prew_ref, wz_ref, wxbc_ref, wdt_ref, convw_ref, convb_ref,
                       dtb_ref, alog_ref, dexp_ref, normw_ref, wout_ref, postw_ref,
                       y_ref, convst_ref, hst_ref,
                       u_scr, xe_scr, xact_scr, b_scr, c_scr, dt_scr, da_scr, yb_scr, st_scr):
    tb = L0_TB
    t = pl.program_id(1)
    nt = pl.num_programs(1)

    @pl.when(t == 0)
    def _():
        xe_scr[0:HALO, :] = jnp.zeros((HALO, SSD_CONV_DIM), F32)
        st_scr[...] = jnp.zeros_like(st_scr)

    x = x_ref[0]
    u = _rms(x, prew_ref[...]).astype(BF16)
    u_scr[...] = u

    nt_cols = 512
    for n0 in range(0, SSD_CONV_DIM, nt_cols):
        xe_scr[HALO:HALO + tb, n0:n0 + nt_cols] = _dot(u, wxbc_ref[:, n0:n0 + nt_cols])

    for n0 in range(0, SSD_CONV_DIM, nt_cols):
        acc = convb_ref[:, n0:n0 + nt_cols]
        for k in range(CONV_WIDTH):
            off = HALO - (CONV_WIDTH - 1) + k
            acc = acc + xe_scr[off:off + tb, n0:n0 + nt_cols] * convw_ref[k:k + 1, n0:n0 + nt_cols]
        act = _silu(acc)
        if n0 < D_INNER:
            xact_scr[:, n0:n0 + nt_cols] = act
        elif n0 < D_INNER + SSD_GN:
            b_scr[:, n0 - D_INNER:n0 - D_INNER + nt_cols] = act
        else:
            c_scr[:, n0 - D_INNER - SSD_GN:n0 - D_INNER - SSD_GN + nt_cols] = act

    @pl.when(t == nt - 1)
    def _():
        convst_ref[0] = xe_scr[HALO + tb - (CONV_WIDTH - 1):HALO + tb, :]

    xe_scr[0:HALO, :] = xe_scr[tb:tb + HALO, :]

    dt = _softplus(_dot(u, wdt_ref[...]) + dtb_ref[...])
    dt_scr[...] = dt
    da_scr[...] = dt * (-jnp.exp(alog_ref[...]))

    q = SSD_CHUNK
    causal = _tri(q)
    tri_f = causal.astype(F32)
    lane = lax.broadcasted_iota(jnp.int32, (q, LANES), 1)
    lane_lo = lane < SSD_HEAD_DIM
    hpg = SSD_HEADS // SSD_GROUPS
    gw = D_INNER // SSD_GROUPS

    def chunk(ci, carry):
        r0 = pl.multiple_of(ci * q, q)
        rows = pl.ds(r0, q)
        cum = _dot_exact(tri_f, da_scr[rows, :])
        cum_t = cum.T
        dtc = dt_scr[rows, :]
        for g in range(SSD_GROUPS):
            bg = b_scr[rows, g * SSD_STATE:(g + 1) * SSD_STATE]
            cg = c_scr[rows, g * SSD_STATE:(g + 1) * SSD_STATE].astype(BF16)
            bg_t = bg.T.astype(BF16)
            cb = _dot(cg, bg_t)
            st_g = st_scr[:, g * gw:(g + 1) * gw]
            y_inter = _dot(cg, st_g.astype(BF16))
            xt_parts = []
            el_parts = []
            for jj in range(hpg // 2):
                j = g * (hpg // 2) + jj
                h0, h1 = 2 * j, 2 * j + 1
                col0 = jnp.broadcast_to(cum[:, h0:h0 + 1], (q, q))
                col1 = jnp.broadcast_to(cum[:, h1:h1 + 1], (q, q))
                row0 = jnp.broadcast_to(cum_t[h0:h0 + 1, :], (q, q))
                row1 = jnp.broadcast_to(cum_t[h1:h1 + 1, :], (q, q))
                dec0 = jnp.exp(jnp.where(causal, col0 - row0, -jnp.inf))
                dec1 = jnp.exp(jnp.where(causal, col1 - row1, -jnp.inf))
                m_pair = jnp.concatenate([(cb * dec0).astype(BF16), (cb * dec1).astype(BF16)], axis=1)
                col_p = jnp.where(lane_lo, col0, col1)
                dt_p = jnp.where(lane_lo,
                                 jnp.broadcast_to(dtc[:, h0:h0 + 1], (q, LANES)),
                                 jnp.broadcast_to(dtc[:, h1:h1 + 1], (q, LANES)))
                cols = slice(j * LANES, (j + 1) * LANES)
                xp = xact_scr[rows, cols]
                xdt = xp * dt_p
                x_bd = jnp.concatenate([jnp.where(lane_lo, xdt, 0.0).astype(BF16),
                                        jnp.where(lane_lo, 0.0, xdt).astype(BF16)], axis=0)
                y_p = _dot(m_pair, x_bd)
                y_p = y_p + y_inter[:, jj * LANES:(jj + 1) * LANES] * jnp.exp(col_p)
                y_p = y_p + dexp_ref[:, cols] * xp
                yb_scr[rows, cols] = y_p
                last_p = col_p[q - 1:q, :]
                xt_parts.append((xdt * jnp.exp(last_p - col_p)).astype(BF16))
                el_parts.append(jnp.exp(last_p))
            xt = jnp.concatenate(xt_parts, axis=1)
            el = jnp.concatenate(el_parts, axis=1)
            st_scr[:, g * gw:(g + 1) * gw] = st_g * el + _dot(bg_t, xt)
        return carry

    lax.fori_loop(0, tb // q, chunk, 0)

    @pl.when(t == nt - 1)
    def _():
        for k in range(D_INNER // LANES):
            hst_ref[0, k * LANES:(k + 1) * LANES, :] = st_scr[:, k * LANES:(k + 1) * LANES].T

    for g in range(SSD_GROUPS):
        cols = slice(g * gw, (g + 1) * gw)
        z = _dot(u_scr[...], wz_ref[:, cols])
        yg = yb_scr[:, cols] * _silu(z)
        yg = _rms(yg, normw_ref[:, cols])
        yb_scr[:, cols] = yg
    out = _dot(yb_scr[...].astype(BF16), wout_ref[...])
    y_ref[0] = x_ref[0] + _rms(out, postw_ref[...])


def _ssd_prompt(x, prew, wz, wxbc, wdt, convw, convb, dtb, alog, dexp, normw, wout, postw):
    bsz, seq, _ = x.shape
    tb = L0_TB
    grid = (bsz, seq // tb)
    consts = (prew, wz, wxbc, wdt, convw, convb, dtb, alog, dexp, normw, wout, postw)
    in_specs = [pl.BlockSpec((1, tb, D_MODEL), lambda b, t: (b, t, 0))]
    in_specs += [_const_spec(c.shape) for c in consts]
    out_shape = (jax.ShapeDtypeStruct((bsz, seq, D_MODEL), F32),
                 jax.ShapeDtypeStruct((bsz, CONV_WIDTH - 1, SSD_CONV_DIM), F32),
                 jax.ShapeDtypeStruct((bsz, D_INNER, SSD_STATE), F32))
    out_specs = (pl.BlockSpec((1, tb, D_MODEL), lambda b, t: (b, t, 0)),
                 pl.BlockSpec((1, CONV_WIDTH - 1, SSD_CONV_DIM), lambda b, t: (b, 0, 0)),
                 pl.BlockSpec((1, D_INNER, SSD_STATE), lambda b, t: (b, 0, 0)))
    scratch = [pltpu.VMEM((tb, D_MODEL), BF16),
               pltpu.VMEM((tb + HALO, SSD_CONV_DIM), F32),
               pltpu.VMEM((tb, D_INNER), F32),
               pltpu.VMEM((tb, SSD_GN), F32),
               pltpu.VMEM((tb, SSD_GN), F32),
               pltpu.VMEM((tb, LANES), F32),
               pltpu.VMEM((tb, LANES), F32),
               pltpu.VMEM((tb, D_INNER), F32),
               pltpu.VMEM((SSD_STATE, D_INNER), F32)]
    return pl.pallas_call(
        _ssd_prompt_kernel, grid=grid, in_specs=in_specs, out_specs=out_specs,
        out_shape=out_shape, scratch_shapes=scratch, name="ssd_prompt",
        compiler_params=pltpu.CompilerParams(
            dimension_semantics=("arbitrary", "arbitrary"), vmem_limit_bytes=VMEM_LIMIT),
    )(x, *consts)


def _pad_lanes(a, n=LANES):
    return jnp.pad(a, ((0, 0), (0, n - a.shape[-1])))


L1_TB = 256


def _log_sigmoid(x):
    return jnp.minimum(x, 0.0) - jnp.log1p(jnp.exp(-jnp.abs(x)))


def _dot_nt(a, b):
    return lax.dot_general(a, b, (((1,), (1,)), ((), ())), preferred_element_type=F32)


def _dot_tn(a, b):
    return lax.dot_general(a, b, (((0,), (0,)), ((), ())), preferred_element_type=F32)


def _mlstm_prompt_kernel(x_ref, prew_ref, wxm_ref, wz_ref, wo_ref, convw_ref, convb_ref,
                         wq_ref, wk_ref, wv_ref, wg_ref, bg_ref, hnw_ref, skip_ref, wout_ref,
                         postw_ref,
                         y_ref, convst_ref, c_ref, n_ref, m_ref,
                         u_scr, xe_scr, xc_scr, qkv_scr, qs_scr, h_scr, hb_scr, n_scr, m_scr):
    tb = L1_TB
    hd_dim = MLSTM_HEAD_DIM
    t = pl.program_id(1)
    nt = pl.num_programs(1)

    @pl.when(t == 0)
    def _():
        xe_scr[0:HALO, :] = jnp.zeros((HALO, D_INNER), F32)
        c_ref[...] = jnp.zeros_like(c_ref)
        n_scr[...] = jnp.zeros_like(n_scr)
        m_scr[...] = jnp.zeros_like(m_scr)

    x = x_ref[0]
    u = _rms(x, prew_ref[...]).astype(BF16)
    u_scr[...] = u

    nt_cols = 512
    for n0 in range(0, D_INNER, nt_cols):
        xe_scr[HALO:HALO + tb, n0:n0 + nt_cols] = _dot(u, wxm_ref[:, n0:n0 + nt_cols])

    for n0 in range(0, D_INNER, nt_cols):
        acc = convb_ref[:, n0:n0 + nt_cols]
        for k in range(CONV_WIDTH):
            off = HALO - (CONV_WIDTH - 1) + k
            acc = acc + xe_scr[off:off + tb, n0:n0 + nt_cols] * convw_ref[k:k + 1, n0:n0 + nt_cols]
        xc_scr[:, n0:n0 + nt_cols] = _silu(acc)

    scale = hd_dim ** -0.5
    for hd in range(MLSTM_HEADS):
        cols = slice(hd * hd_dim, (hd + 1) * hd_dim)
        xc_h = xc_scr[:, cols].astype(BF16)
        xm_h = xe_scr[HALO:HALO + tb, cols].astype(BF16)
        q_h = _dot(xc_h, wq_ref[hd])
        qkv_scr[:, cols] = q_h.astype(BF16)
        qs_scr[:, cols] = (q_h * scale).astype(BF16)
        qkv_scr[:, D_INNER + hd * hd_dim:D_INNER + (hd + 1) * hd_dim] = _dot(xc_h, wk_ref[hd]).astype(BF16)
        qkv_scr[:, 2 * D_INNER + hd * hd_dim:2 * D_INNER + (hd + 1) * hd_dim] = _dot(xm_h, wv_ref[hd]).astype(BF16)

    @pl.when(t == nt - 1)
    def _():
        convst_ref[0] = xe_scr[HALO + tb - (CONV_WIDTH - 1):HALO + tb, :]

    xe_scr[0:HALO, :] = xe_scr[tb:tb + HALO, :]

    gates = _dot(qkv_scr[...], wg_ref[...]) + bg_ref[...]
    li_all = gates
    lf_all = pltpu.roll(_log_sigmoid(gates), LANES - MLSTM_HEADS, axis=1)
    causal = _tri(tb)
    bcum = _dot_exact(causal.astype(F32), lf_all)
    m_row = m_scr[0:1, :]
    inter_all = bcum + m_row
    bcum_t = bcum.T
    li_t = li_all.T
    lane_row = lax.broadcasted_iota(jnp.int32, (1, LANES), 1)
    m_new_row = m_row

    for hd in range(MLSTM_HEADS):
        cols = slice(hd * hd_dim, (hd + 1) * hd_dim)
        kcols = slice(D_INNER + hd * hd_dim, D_INNER + (hd + 1) * hd_dim)
        vcols = slice(2 * D_INNER + hd * hd_dim, 2 * D_INNER + (hd + 1) * hd_dim)
        b_col = bcum[:, hd:hd + 1]
        dmat = b_col + (li_t[hd:hd + 1, :] - bcum_t[hd:hd + 1, :])
        dmat = jnp.where(causal, dmat, -jnp.inf)
        inter = inter_all[:, hd:hd + 1]
        m_t = jnp.maximum(inter, jnp.max(dmat, axis=1, keepdims=True))
        w = jnp.exp(dmat - m_t)
        qs_h = qs_scr[:, cols]
        k_h = qkv_scr[:, kcols]
        v_h = qkv_scr[:, vcols]
        s = _dot_nt(qs_h, k_h) * w
        inter_w = jnp.exp(inter - m_t)
        c_old = c_ref[0, hd]
        n_row = n_scr[0:1, cols]
        num = _dot(s.astype(BF16), v_h) + inter_w * _dot(qs_h, c_old.astype(BF16))
        qn = jnp.sum(qs_h.astype(F32) * n_row, axis=1, keepdims=True)
        den = jnp.sum(s, axis=1, keepdims=True) + inter_w * qn
        hh = num / jnp.maximum(jnp.abs(den), jnp.exp(-m_t))
        hc = hh - jnp.mean(hh, axis=1, keepdims=True)
        h_scr[:, cols] = hc * lax.rsqrt(jnp.mean(hc * hc, axis=1, keepdims=True) + EPS)
        m_new = m_t[tb - 1:tb, :]
        b_last = b_col[tb - 1:tb, :]
        wts = jnp.exp(b_last - b_col + li_all[:, hd:hd + 1] - m_new)
        cd = jnp.exp(b_last + m_row[:, hd:hd + 1] - m_new)
        kw = k_h.astype(F32) * wts
        c_ref[0, hd] = cd * c_old + _dot_tn(kw.astype(BF16), v_h)
        n_scr[0:1, cols] = cd * n_row + jnp.sum(kw, axis=0, keepdims=True)
        m_new_row = jnp.where(lane_row == hd, m_new, m_new_row)

    m_scr[0:1, :] = m_new_row

    @pl.when(t == nt - 1)
    def _():
        n_ref[0] = n_scr[0:1, :]
        m_ref[0] = m_new_row

    for n0 in range(0, D_INNER, nt_cols):
        cols = slice(n0, n0 + nt_cols)
        o = _sigmoid(_dot(u_scr[...], wo_ref[:, cols]))
        z = _dot(u_scr[...], wz_ref[:, cols])
        hg = o * (h_scr[:, cols] * hnw_ref[:, cols]) + skip_ref[:, cols] * xc_scr[:, cols]
        hb_scr[:, cols] = (hg * _silu(z)).astype(BF16)
    out = _dot(hb_scr[...], wout_ref[...])
    y_ref[0] = x_ref[0] + _rms(out, postw_ref[...])


def _mlstm_prompt(x, prew, wxm, wz, wo, convw, convb, wq, wk, wv, wg, bg, hnw, skip, wout, postw):
    bsz, seq, _ = x.shape
    tb = L1_TB
    grid = (bsz, seq // tb)
    consts = (prew, wxm, wz, wo, convw, convb, wq, wk, wv, wg, bg, hnw, skip, wout, postw)
    in_specs = [pl.BlockSpec((1, tb, D_MODEL), lambda b, t: (b, t, 0))]
    in_specs += [_const_spec(c.shape) for c in consts]
    hd_dim = MLSTM_HEAD_DIM
    out_shape = (jax.ShapeDtypeStruct((bsz, seq, D_MODEL), F32),
                 jax.ShapeDtypeStruct((bsz, CONV_WIDTH - 1, D_INNER), F32),
                 jax.ShapeDtypeStruct((bsz, MLSTM_HEADS, hd_dim, hd_dim), F32),
                 jax.ShapeDtypeStruct((bsz, 1, D_INNER), F32),
                 jax.ShapeDtypeStruct((bsz, 1, LANES), F32))
    out_specs = (pl.BlockSpec((1, tb, D_MODEL), lambda b, t: (b, t, 0)),
                 pl.BlockSpec((1, CONV_WIDTH - 1, D_INNER), lambda b, t: (b, 0, 0)),
                 pl.BlockSpec((1, MLSTM_HEADS, hd_dim, hd_dim), lambda b, t: (b, 0, 0, 0)),
                 pl.BlockSpec((1, 1, D_INNER), lambda b, t: (b, 0, 0)),
                 pl.BlockSpec((1, 1, LANES), lambda b, t: (b, 0, 0)))
    scratch = [pltpu.VMEM((tb, D_MODEL), BF16),
               pltpu.VMEM((tb + HALO, D_INNER), F32),
               pltpu.VMEM((tb, D_INNER), F32),
               pltpu.VMEM((tb, 3 * D_INNER), BF16),
               pltpu.VMEM((tb, D_INNER), BF16),
               pltpu.VMEM((tb, D_INNER), F32),
               pltpu.VMEM((tb, D_INNER), BF16),
               pltpu.VMEM((SUBLANES, D_INNER), F32),
               pltpu.VMEM((SUBLANES, LANES), F32)]
    return pl.pallas_call(
        _mlstm_prompt_kernel, grid=grid, in_specs=in_specs, out_specs=out_specs,
        out_shape=out_shape, scratch_shapes=scratch, name="mlstm_prompt",
        compiler_params=pltpu.CompilerParams(
            dimension_semantics=("arbitrary", "arbitrary"), vmem_limit_bytes=VMEM_LIMIT),
    )(x, *consts)


N_SAMPLE = 128
S0_SEQS = 4


def _split3(v):
    hi = v.astype(BF16)
    r1 = v - hi.astype(F32)
    mid = r1.astype(BF16)
    lo = (r1 - mid.astype(F32)).astype(BF16)
    return hi, mid, lo


def _expand_heads(v, width):
    rr = lax.broadcasted_iota(jnp.int32, (3 * LANES, D_INNER), 0) & (LANES - 1)
    cc = lax.broadcasted_iota(jnp.int32, (3 * LANES, D_INNER), 1) // width
    e3 = jnp.where(rr == cc, 1.0, 0.0).astype(BF16)
    return _dot(jnp.concatenate(_split3(v), axis=1), e3)


def _ssd_sample_pre_kernel(x_ref, cs_ref, prew_ref, wz_ref, wxbc_ref, wdt_ref, convw_ref, convb_ref,
                           dtb_ref, alog_ref,
                           z_ref, xact_ref, b_ref, ct_ref, xdtt_ref, dec_ref, csnew_ref):
    cd = SSD_CONV_DIM
    u = _rms(x_ref[...], prew_ref[...]).astype(BF16)
    xbc = _dot(u, wxbc_ref[...])
    acc = convb_ref[...] + xbc * convw_ref[CONV_WIDTH - 1:CONV_WIDTH, :]
    for k in range(CONV_WIDTH - 1):
        acc = acc + cs_ref[:, k * cd:(k + 1) * cd] * convw_ref[k:k + 1, :]
    act = _silu(acc)
    csnew_ref[:, 0:2 * cd] = cs_ref[:, cd:3 * cd]
    csnew_ref[:, 2 * cd:3 * cd] = xbc
    xact = act[:, :D_INNER]
    dt = _softplus(_dot(u, wdt_ref[...]) + dtb_ref[...])
    dec_ref[...] = jnp.exp(dt * (-jnp.exp(alog_ref[...])))
    xdt = xact * _expand_heads(dt, SSD_HEAD_DIM)
    xdtt_ref[...] = xdt.T.astype(BF16)
    z_ref[...] = _dot(u, wz_ref[...])
    xact_ref[...] = xact
    b_ref[...] = act[:, D_INNER:D_INNER + SSD_GN]
    ct_ref[...] = act[:, D_INNER + SSD_GN:].T


def _ssd_sample_pre(x, cs, prew, wz, wxbc, wdt, convw, convb, dtb, alog):
    n = x.shape[0]
    out_shape = (jax.ShapeDtypeStruct((n, D_INNER), F32),
                 jax.ShapeDtypeStruct((n, D_INNER), F32),
                 jax.ShapeDtypeStruct((n, SSD_GN), F32),
                 jax.ShapeDtypeStruct((SSD_GN, n), F32),
                 jax.ShapeDtypeStruct((D_INNER, n), BF16),
                 jax.ShapeDtypeStruct((n, LANES), F32),
                 jax.ShapeDtypeStruct((n, (CONV_WIDTH - 1) * SSD_CONV_DIM), F32))
    return pl.pallas_call(
        _ssd_sample_pre_kernel, out_shape=out_shape, name="ssd_sample_pre",
        compiler_params=pltpu.CompilerParams(vmem_limit_bytes=VMEM_LIMIT),
    )(x, cs, prew, wz, wxbc, wdt, convw, convb, dtb, alog)


def _ssd_sample_state_kernel(dec_ref, h_ref, xdtt_ref, b_ref, ct_ref, hnew_ref, yt_ref):
    j = pl.program_id(0)
    gw = D_INNER // SSD_GROUPS
    hpg = SSD_HEADS // SSD_GROUPS

    @pl.when(j == 0)
    def _():
        yt_ref[...] = jnp.zeros_like(yt_ref)

    rowid = lax.broadcasted_iota(jnp.int32, (N_SAMPLE, SSD_STATE), 0)
    colid = lax.broadcasted_iota(jnp.int32, (SSD_STATE, N_SAMPLE), 1)
    for pp in range(S0_SEQS // 2):
        seqs = (j * S0_SEQS + 2 * pp, j * S0_SEQS + 2 * pp + 1)
        for g in range(SSD_GROUPS):
            bg = b_ref[:, g * SSD_STATE:(g + 1) * SSD_STATE]
            ctg = ct_ref[g * SSD_STATE:(g + 1) * SSD_STATE, :]
            rb = jnp.concatenate([jnp.where(rowid == s, bg, 0.0) for s in seqs], axis=1).astype(BF16)
            upd = _dot(xdtt_ref[g * gw:(g + 1) * gw, :], rb)
            for i, s in enumerate(seqs):
                for r in range(hpg):
                    rows = slice(g * gw + r * SSD_HEAD_DIM, g * gw + (r + 1) * SSD_HEAD_DIM)
                    dec = dec_ref[s * SSD_HEADS + g * hpg + r]
                    hn = (h_ref[2 * pp + i, rows, :] * dec
                          + upd[r * SSD_HEAD_DIM:(r + 1) * SSD_HEAD_DIM, i * SSD_STATE:(i + 1) * SSD_STATE])
                    hnew_ref[2 * pp + i, rows, :] = hn
            hn_pair = jnp.concatenate(
                [hnew_ref[2 * pp + i, g * gw:(g + 1) * gw, :] for i in range(2)], axis=1).astype(BF16)
            wc = jnp.concatenate([jnp.where(colid == s, ctg, 0.0) for s in seqs], axis=0).astype(BF16)
            yt_ref[g * gw:(g + 1) * gw, :] += _dot(hn_pair, wc)


def _ssd_sample_state(dec_flat, h, xdtt, b, ct):
    n = h.shape[0]
    grid = (n // S0_SEQS,)
    in_specs = [pl.BlockSpec(memory_space=pltpu.SMEM),
                pl.BlockSpec((S0_SEQS, D_INNER, SSD_STATE), lambda j: (j, 0, 0)),
                _const_spec(xdtt.shape), _const_spec(b.shape), _const_spec(ct.shape)]
    out_shape = (jax.ShapeDtypeStruct(h.shape, F32),
                 jax.ShapeDtypeStruct((D_INNER, n), F32))
    out_specs = (pl.BlockSpec((S0_SEQS, D_INNER, SSD_STATE), lambda j: (j, 0, 0)),
                 pl.BlockSpec((D_INNER, n), lambda j: (0, 0)))
    return pl.pallas_call(
        _ssd_sample_state_kernel, grid=grid, in_specs=in_specs, out_specs=out_specs,
        out_shape=out_shape, name="ssd_sample_state",
        compiler_params=pltpu.CompilerParams(
            dimension_semantics=("arbitrary",), vmem_limit_bytes=VMEM_LIMIT),
    )(dec_flat, h, xdtt, b, ct)


def _head_cols(a, hd):
    return a[:, hd * MLSTM_HEAD_DIM:(hd + 1) * MLSTM_HEAD_DIM]


def _sample_mid_kernel(x_ref, yt_ref, xact_ref, z_ref, dexp_ref, normw_ref, wout0_ref, postw0_ref,
                       mcs_ref, n0_ref, m0_ref, prew1_ref, wxm_ref, wz1_ref, wo1_ref, convw_ref,
                       convb_ref, wq_ref, wk_ref, wv_ref, wg_ref, bg_ref,
                       x1_ref, z1_ref, o1_ref, xc_ref, v_ref, kt_ref, wv_out_ref, qs_ref,
                       s_ref, iw_ref, hden_ref, mcsnew_ref, nnew_ref, mnew_ref):
    gw = D_INNER // SSD_GROUPS
    y = yt_ref[...].T + dexp_ref[...] * xact_ref[...]
    y = y * _silu(z_ref[...])
    parts = []
    for g in range(SSD_GROUPS):
        cols = slice(g * gw, (g + 1) * gw)
        parts.append(_rms(y[:, cols], normw_ref[:, cols]).astype(BF16))
    out0 = _dot(jnp.concatenate(parts, axis=1), wout0_ref[...])
    x1 = x_ref[...] + _rms(out0, postw0_ref[...])
    x1_ref[...] = x1

    u = _rms(x1, prew1_ref[...]).astype(BF16)
    xm = _dot(u, wxm_ref[...])
    z1_ref[...] = _dot(u, wz1_ref[...])
    o1_ref[...] = _dot(u, wo1_ref[...])
    acc = convb_ref[...] + xm * convw_ref[CONV_WIDTH - 1:CONV_WIDTH, :]
    for k in range(CONV_WIDTH - 1):
        acc = acc + mcs_ref[:, k * D_INNER:(k + 1) * D_INNER] * convw_ref[k:k + 1, :]
    xc = _silu(acc)
    xc_ref[...] = xc
    mcsnew_ref[:, 0:2 * D_INNER] = mcs_ref[:, D_INNER:3 * D_INNER]
    mcsnew_ref[:, 2 * D_INNER:3 * D_INNER] = xm

    scale = MLSTM_HEAD_DIM ** -0.5
    qs, ks, vs = [], [], []
    for hd in range(MLSTM_HEADS):
        xc_h = _head_cols(xc, hd).astype(BF16)
        xm_h = _head_cols(xm, hd).astype(BF16)
        qs.append(_dot(xc_h, wq_ref[hd]))
        ks.append(_dot(xc_h, wk_ref[hd]))
        vs.append(_dot(xm_h, wv_ref[hd]))
    qkv = jnp.concatenate(qs + ks + vs, axis=1).astype(BF16)
    gates = _dot(qkv, wg_ref[...]) + bg_ref[...]
    li = gates
    lf = pltpu.roll(_log_sigmoid(gates), LANES - MLSTM_HEADS, axis=1)
    m0 = m0_ref[...]
    inter = lf + m0
    m_t = jnp.maximum(inter, li)
    w = jnp.exp(li - m_t)
    iw = jnp.exp(inter - m_t)
    mnew_ref[...] = m_t

    lane = lax.broadcasted_iota(jnp.int32, (N_SAMPLE, LANES), 1)
    qk_arr = jnp.zeros((N_SAMPLE, LANES), F32)
    qn_arr = jnp.zeros((N_SAMPLE, LANES), F32)
    kb_parts = []
    for hd in range(MLSTM_HEADS):
        cols = slice(hd * MLSTM_HEAD_DIM, (hd + 1) * MLSTM_HEAD_DIM)
        q_h = (qs[hd] * scale).astype(BF16).astype(F32)
        k_h = ks[hd].astype(BF16).astype(F32)
        n_h = n0_ref[:, cols]
        qk_arr = jnp.where(lane == hd, jnp.sum(q_h * k_h, axis=1, keepdims=True), qk_arr)
        qn_arr = jnp.where(lane == hd, jnp.sum(q_h * n_h, axis=1, keepdims=True), qn_arr)
        w_h = w[:, hd:hd + 1]
        iw_h = iw[:, hd:hd + 1]
        nnew_ref[:, cols] = iw_h * n_h + w_h * k_h
        wv_out_ref[:, cols] = w_h * vs[hd]
        v_ref[:, cols] = vs[hd]
        qs_ref[:, cols] = q_h
        kb_parts.append(k_h)
    kt_ref[...] = jnp.concatenate(kb_parts, axis=1).T.astype(BF16)
    s = qk_arr * w
    den = s + iw * qn_arr
    s_ref[...] = s
    iw_ref[...] = iw
    hden_ref[...] = jnp.maximum(jnp.abs(den), jnp.exp(-m_t))


def _sample_mid(*args):
    n = N_SAMPLE
    wide = jax.ShapeDtypeStruct((n, D_INNER), F32)
    small = jax.ShapeDtypeStruct((n, LANES), F32)
    out_shape = (jax.ShapeDtypeStruct((n, D_MODEL), F32),
                 wide, wide, wide, wide,
                 jax.ShapeDtypeStruct((D_INNER, n), BF16),
                 wide, wide,
                 small, small, small,
                 jax.ShapeDtypeStruct((n, (CONV_WIDTH - 1) * D_INNER), F32),
                 wide, small)
    return pl.pallas_call(
        _sample_mid_kernel, out_shape=out_shape, name="sample_mid",
        compiler_params=pltpu.CompilerParams(vmem_limit_bytes=VMEM_LIMIT),
    )(*args)


def _mlstm_sample_state_kernel(cd_ref, c_ref, kt_ref, wv_ref, qs_ref, cnew_ref, num_ref):
    b = pl.program_id(0)
    rowid = lax.broadcasted_iota(jnp.int32, (N_SAMPLE, MLSTM_HEAD_DIM), 0)
    sub = lax.broadcasted_iota(jnp.int32, (SUBLANES, MLSTM_HEAD_DIM), 0)
    b8 = pl.multiple_of((b // SUBLANES) * SUBLANES, SUBLANES)
    for hd in range(MLSTM_HEADS):
        cols = slice(hd * MLSTM_HEAD_DIM, (hd + 1) * MLSTM_HEAD_DIM)
        c_old = c_ref[0, hd]
        rv = jnp.where(rowid == b, wv_ref[:, cols], 0.0).astype(BF16)
        upd = _dot(kt_ref[cols, :], rv)
        cnew_ref[0, hd] = cd_ref[b * MLSTM_HEADS + hd] * c_old + upd
        q8 = qs_ref[pl.ds(b8, SUBLANES), cols].astype(BF16)
        r8 = _dot(q8, c_old.astype(BF16))
        num_ref[pl.ds(b, 1), cols] = jnp.sum(jnp.where(sub == b - b8, r8, 0.0), axis=0, keepdims=True)


def _mlstm_sample_state(cd_flat, c, kt, wv, qs):
    n = c.shape[0]
    blk = (1, MLSTM_HEADS, MLSTM_HEAD_DIM, MLSTM_HEAD_DIM)
    in_specs = [pl.BlockSpec(memory_space=pltpu.SMEM),
                pl.BlockSpec(blk, lambda b: (b, 0, 0, 0)),
                _const_spec(kt.shape), _const_spec(wv.shape), _const_spec(qs.shape)]
    out_shape = (jax.ShapeDtypeStruct(c.shape, F32), jax.ShapeDtypeStruct((n, D_INNER), F32))
    out_specs = (pl.BlockSpec(blk, lambda b: (b, 0, 0, 0)),
                 pl.BlockSpec((n, D_INNER), lambda b: (0, 0)))
    return pl.pallas_call(
        _mlstm_sample_state_kernel, grid=(n,), in_specs=in_specs, out_specs=out_specs,
        out_shape=out_shape, name="mlstm_sample_state",
        compiler_params=pltpu.CompilerParams(
            dimension_semantics=("arbitrary",), vmem_limit_bytes=VMEM_LIMIT),
    )(cd_flat, c, kt, wv, qs)


def _mlstm_sample_post_kernel(x1_ref, num_ref, v_ref, s_ref, iw_ref, hden_ref, o1_ref, z1_ref,
                              xc_ref, hnw_ref, skip_ref, wout_ref, postw_ref, y_ref):
    parts = []
    for hd in range(MLSTM_HEADS):
        cols = slice(hd * MLSTM_HEAD_DIM, (hd + 1) * MLSTM_HEAD_DIM)
        num = s_ref[:, hd:hd + 1] * v_ref[:, cols] + iw_ref[:, hd:hd + 1] * num_ref[:, cols]
        hh = num / hden_ref[:, hd:hd + 1]
        hc = hh - jnp.mean(hh, axis=1, keepdims=True)
        hn = hc * lax.rsqrt(jnp.mean(hc * hc, axis=1, keepdims=True) + EPS)
        hg = _sigmoid(o1_ref[:, cols]) * (hn * hnw_ref[:, cols]) + skip_ref[:, cols] * xc_ref[:, cols]
        parts.append((hg * _silu(z1_ref[:, cols])).astype(BF16))
    out = _dot(jnp.concatenate(parts, axis=1), wout_ref[...])
    y_ref[...] = x1_ref[...] + _rms(out, postw_ref[...])


def _mlstm_sample_post(*args):
    return pl.pallas_call(
        _mlstm_sample_post_kernel, out_shape=jax.ShapeDtypeStruct((N_SAMPLE, D_MODEL), F32),
        name="mlstm_sample_post",
        compiler_params=pltpu.CompilerParams(vmem_limit_bytes=VMEM_LIMIT),
    )(*args)


def kernel(x_prompt, x_sample, state_ssd_conv, state_ssd, state_mlstm_conv, state_mlstm_c, state_mlstm_n, state_mlstm_m, pre_norm_w, post_norm_w, ssd_w_in, ssd_conv_w, ssd_conv_b, ssd_dt_bias, ssd_a_log, ssd_d, ssd_norm_w, ssd_w_out, ml_w_in, ml_conv_w, ml_conv_b, ml_w_q, ml_w_k, ml_w_v, ml_w_gate, ml_b_gate, ml_head_norm_w, ml_skip, ml_w_out):
    w_in = ssd_w_in[0]
    wz = w_in[:, :D_INNER].astype(BF16)
    wxbc = w_in[:, D_INNER:D_INNER + SSD_CONV_DIM].astype(BF16)
    wdt = _pad_lanes(w_in[:, D_INNER + SSD_CONV_DIM:]).astype(BF16)
    dtb = _pad_lanes(ssd_dt_bias[0][None])
    alog = _pad_lanes(ssd_a_log[0][None])
    dexp = jnp.repeat(ssd_d[0], SSD_HEAD_DIM)[None]
    prew0, postw0 = pre_norm_w[0][None], post_norm_w[0][None]
    prew1, postw1 = pre_norm_w[1][None], post_norm_w[1][None]
    convw0, convb0 = ssd_conv_w[0], ssd_conv_b[0][None]
    normw0 = ssd_norm_w[0][None]
    wout0 = ssd_w_out[0].astype(BF16)
    mw_in = ml_w_in[0]
    wxm = mw_in[:, :D_INNER].astype(BF16)
    mwz = mw_in[:, D_INNER:2 * D_INNER].astype(BF16)
    mwo = mw_in[:, 2 * D_INNER:].astype(BF16)
    wg = _pad_lanes(ml_w_gate[0]).astype(BF16)
    bg = _pad_lanes(ml_b_gate[0][None])
    convw1, convb1 = ml_conv_w[0], ml_conv_b[0][None]
    wq, wk, wv = ml_w_q[0].astype(BF16), ml_w_k[0].astype(BF16), ml_w_v[0].astype(BF16)
    hnw, skip = ml_head_norm_w[0][None], ml_skip[0][None]
    wout1 = ml_w_out[0].astype(BF16)

    bp = x_prompt.shape[0]
    y0, p_sc, p_sh = _ssd_prompt(x_prompt, prew0, wz, wxbc, wdt, convw0, convb0, dtb, alog, dexp,
                                 normw0, wout0, postw0)
    y_prompt, p_mc, p_c, p_n, p_m = _mlstm_prompt(
        y0, prew1, wxm, mwz, mwo, convw1, convb1, wq, wk, wv, wg, bg, hnw, skip, wout1, postw1)
    p_sc = p_sc[None]
    p_sh = p_sh.reshape(1, bp, SSD_HEADS, SSD_HEAD_DIM, SSD_STATE)
    p_mc = p_mc[None]
    p_c = p_c[None]
    p_n = p_n.reshape(1, bp, MLSTM_HEADS, MLSTM_HEAD_DIM)
    p_m = p_m[:, 0, :MLSTM_HEADS][None]

    n = x_sample.shape[0]
    xs = x_sample.reshape(n, D_MODEL)
    cs0 = state_ssd_conv[0].reshape(n, (CONV_WIDTH - 1) * SSD_CONV_DIM)
    z0, xact0, b0, ct0, xdtt0, dec0, csnew0 = _ssd_sample_pre(
        xs, cs0, prew0, wz, wxbc, wdt, convw0, convb0, dtb, alog)
    h0 = state_ssd[0].reshape(n, D_INNER, SSD_STATE)
    hnew0, yt0 = _ssd_sample_state(dec0[:, :SSD_HEADS].reshape(-1), h0, xdtt0, b0, ct0)
    mcs = state_mlstm_conv[0].reshape(n, (CONV_WIDTH - 1) * D_INNER)
    n0 = state_mlstm_n[0].reshape(n, D_INNER)
    m0 = _pad_lanes(state_mlstm_m[0])
    (x1, z1, o1, xc1, v1, kt1, wv1, qs1, s1, iw1, hden1, mcsnew, nnew, mnew) = _sample_mid(
        xs, yt0, xact0, z0, dexp, normw0, wout0, postw0,
        mcs, n0, m0, prew1, wxm, mwz, mwo, convw1, convb1, wq, wk, wv, wg, bg)
    cnew, num1 = _mlstm_sample_state(iw1[:, :MLSTM_HEADS].reshape(-1), state_mlstm_c[0], kt1, wv1, qs1)
    y_s = _mlstm_sample_post(x1, num1, v1, s1, iw1, hden1, o1, z1, xc1, hnw, skip, wout1, postw1)

    y_sample = y_s.reshape(n, 1, D_MODEL)
    s_sc = csnew0.reshape(1, n, CONV_WIDTH - 1, SSD_CONV_DIM)
    s_sh = hnew0.reshape(1, n, SSD_HEADS, SSD_HEAD_DIM, SSD_STATE)
    s_mc = mcsnew.reshape(1, n, CONV_WIDTH - 1, D_INNER)
    s_c = cnew[None]
    s_n = nnew.reshape(1, n, MLSTM_HEADS, MLSTM_HEAD_DIM)
    s_m = mnew[:, :MLSTM_HEADS][None]
    return (y_prompt, y_sample, p_sc, s_sc, p_sh, s_sh, p_mc, s_mc, p_c, s_c, p_n, s_n, p_m, s_m)
```

```python
import jax
import jax.numpy as jnp
from jax import lax
from jax.experimental import pallas as pl
from jax.experimental.pallas import tpu as pltpu

F32 = jnp.float32
BF16 = jnp.bfloat16

D_MODEL = 1024
D_INNER = 2048
CONV_WIDTH = 4
SSD_HEAD_DIM = 64
SSD_HEADS = 32
SSD_GROUPS = 4
SSD_STATE = 128
SSD_GN = SSD_GROUPS * SSD_STATE
SSD_CONV_DIM = D_INNER + 2 * SSD_GN
SSD_CHUNK = 128
MLSTM_HEADS = 4
MLSTM_HEAD_DIM = 512
EPS = 1e-6

LANES = 128
SUBLANES = 8
HALO = SUBLANES
VMEM_LIMIT = 58 * 1024 * 1024

L0_TB = 256


def _sigmoid(x):
    return 1.0 / (1.0 + jnp.exp(-x))


def _silu(x):
    return x * _sigmoid(x)


def _softplus(x):
    return jnp.maximum(x, 0.0) + jnp.log1p(jnp.exp(-jnp.abs(x)))


def _rms(x, w):
    ms = jnp.mean(x * x, axis=-1, keepdims=True)
    return x * lax.rsqrt(ms + EPS) * w


def _dot(a, b):
    return jnp.dot(a, b, preferred_element_type=F32)


def _dot_exact(a, b):
    return jnp.dot(a, b, preferred_element_type=F32, precision=lax.Precision.HIGHEST)


def _tri(n):
    r = lax.broadcasted_iota(jnp.int32, (n, n), 0)
    c = lax.broadcasted_iota(jnp.int32, (n, n), 1)
    return r >= c


WRAP = (CONV_WIDTH - 1) * SUBLANES
CONV_TILE = 512
LOG2E = 1.4426950408889634


def _stage_columns(x_ref, xs_scr):
    for k in range(xs_scr.shape[0]):
        xs_scr[k] = x_ref[0, :, k * LANES:(k + 1) * LANES]


def _load_permuted(xs_scr, tb, q):
    nv = q // SUBLANES
    cols = []
    for k in range(xs_scr.shape[0]):
        parts = [xs_scr[k, pl.ds(c * q + j, SUBLANES, stride=nv), :]
                 for c in range(tb // q) for j in range(nv)]
        cols.append(jnp.concatenate(parts, axis=0))
    return jnp.concatenate(cols, axis=1)


def _store_residual(y_ref, x_ref, xs_scr, branch, tb, q):
    nv = q // SUBLANES
    for k in range(xs_scr.shape[0]):
        for c in range(tb // q):
            for j in range(nv):
                r0 = c * q + j * SUBLANES
                xs_scr[k, pl.ds(c * q + j, SUBLANES, stride=nv), :] = branch[r0:r0 + SUBLANES,
                                                                             k * LANES:(k + 1) * LANES]
    for k in range(xs_scr.shape[0]):
        cols = slice(k * LANES, (k + 1) * LANES)
        y_ref[0, :, cols] = x_ref[0, :, cols] + xs_scr[k]


def _perm_causal(n, q):
    nv = q // SUBLANES
    r = lax.broadcasted_iota(jnp.int32, (n, n), 0)
    c = lax.broadcasted_iota(jnp.int32, (n, n), 1)
    tok = lambda p: (p & -q) + (p & (SUBLANES - 1)) * nv + ((p & (q - 1)) >> 3)
    return tok(r) >= tok(c)


def _fill_wrap_rows(xe_t, hist_t, nchunk, q):
    sub = lax.broadcasted_iota(jnp.int32, (SUBLANES, xe_t.shape[-1]), 0)
    for c in range(nchunk):
        for i in range(CONV_WIDTH - 1):
            src = slice(q + i * SUBLANES, q + (i + 1) * SUBLANES)
            cur = xe_t[c, src, :]
            prev = hist_t[i * SUBLANES:(i + 1) * SUBLANES, :] if c == 0 else xe_t[c - 1, src, :]
            xe_t[c, i * SUBLANES:(i + 1) * SUBLANES, :] = jnp.where(
                sub == 0, pltpu.roll(prev, 1, axis=0), pltpu.roll(cur, 1, axis=0))
    hist_t[...] = xe_t[nchunk - 1, q:q + WRAP, :]


def _conv_tile(xe_t, convw_ref, convb_ref, cols, c, q):
    acc = convb_ref[:, cols]
    for k in range(CONV_WIDTH):
        acc = acc + xe_t[c, k * SUBLANES:k * SUBLANES + q, :] * convw_ref[k:k + 1, cols]
    return acc


def _store_conv_state(convst_ref, xe_scr, nchunk, q, tile):
    for n in range(len(xe_scr)):
        for i in range(CONV_WIDTH - 1):
            row = q + (i + 1) * SUBLANES - 1
            convst_ref[0, i:i + 1, n * tile:(n + 1) * tile] = xe_scr[n][nchunk - 1, row:row + 1, :]


def _ssd_win_views(win_ref):
    return win_ref.at[:, 0:D_INNER], win_ref.at[:, D_INNER:D_INNER + SSD_CONV_DIM]


def _mlstm_win_views(win_ref):
    return (win_ref.at[:, 0:D_INNER], win_ref.at[:, D_INNER:2 * D_INNER],
            win_ref.at[:, 2 * D_INNER:3 * D_INNER])


def _const_spec(shape):
    nd = len(shape)
    return pl.BlockSpec(shape, lambda *_: (0,) * nd, pipeline_mode=pl.Buffered(1))


def _ssd_prompt_kernel(x_ref, prew_ref, win_ref, wdt_ref, convw_ref, convb_ref,
                       dtb_ref, alog_ref, dexp_ref, normw_ref, wout_ref, postw_ref,
                       y_ref, convst_ref, hst_ref,
                       xs_scr, u_scr, hist_scr, xact_scr, b_scr, c_scr, dt_scr, da_scr, yb_scr,
                       st_scr, *xe_scr):
    tb = L0_TB
    t = pl.program_id(1)
    nt = pl.num_programs(1)
    wz_ref, wxbc_ref = _ssd_win_views(win_ref)

    q = SSD_CHUNK
    nchunk = tb // q

    @pl.when(t == 0)
    def _():
        hist_scr[...] = jnp.zeros_like(hist_scr)
        st_scr[...] = jnp.zeros_like(st_scr)

    _stage_columns(x_ref, xs_scr)
    x = _load_permuted(xs_scr, tb, q)
    u = _rms(x, prew_ref[...]).astype(BF16)
    u_scr[...] = u

    for n in range(SSD_CONV_DIM // CONV_TILE):
        n0 = n * CONV_TILE
        cols = slice(n0, n0 + CONV_TILE)
        xe_t = xe_scr[n]
        res = _dot(u, wxbc_ref[:, cols])
        for c in range(nchunk):
            xe_t[c, WRAP:WRAP + q, :] = res[c * q:(c + 1) * q]
        _fill_wrap_rows(xe_t, hist_scr.at[n], nchunk, q)
        for c in range(nchunk):
            rows = slice(c * q, (c + 1) * q)
            act = _silu(_conv_tile(xe_t, convw_ref, convb_ref, cols, c, q))
            if n0 < D_INNER:
                xact_scr[rows, cols] = act
            elif n0 < D_INNER + SSD_GN:
                b_scr[rows, n0 - D_INNER:n0 - D_INNER + CONV_TILE] = act
            else:
                c_scr[rows, n0 - D_INNER - SSD_GN:n0 - D_INNER - SSD_GN + CONV_TILE] = act

    @pl.when(t == nt - 1)
    def _():
        _store_conv_state(convst_ref, xe_scr, nchunk, q, CONV_TILE)

    dt = _softplus(_dot(u, wdt_ref[...]) + dtb_ref[...])
    dt_scr[...] = dt
    da_scr[...] = dt * (-jnp.exp(alog_ref[...])) * LOG2E

    causal = _perm_causal(q, q)
    tri_f = causal.astype(F32)
    lane = lax.broadcasted_iota(jnp.int32, (q, LANES), 1)
    lane_lo = lane < SSD_HEAD_DIM
    hpg = SSD_HEADS // SSD_GROUPS
    gw = D_INNER // SSD_GROUPS

    for ci in range(nchunk):
        rows = slice(ci * q, (ci + 1) * q)
        cum = _dot_exact(tri_f, da_scr[rows, :])
        cum_t = cum.T
        dtc = dt_scr[rows, :]
        for g in range(SSD_GROUPS):
            bg = b_scr[rows, g * SSD_STATE:(g + 1) * SSD_STATE]
            cg = c_scr[rows, g * SSD_STATE:(g + 1) * SSD_STATE].astype(BF16)
            bg_t = bg.T.astype(BF16)
            cb = _dot(cg, bg_t)
            st_g = st_scr[:, g * gw:(g + 1) * gw]
            y_inter = _dot(cg, st_g.astype(BF16))
            xt_parts = []
            el_parts = []
            for jj in range(hpg // 2):
                j = g * (hpg // 2) + jj
                h0, h1 = 2 * j, 2 * j + 1
                col0 = jnp.broadcast_to(cum[:, h0:h0 + 1], (q, q))
                col1 = jnp.broadcast_to(cum[:, h1:h1 + 1], (q, q))
                row0 = jnp.broadcast_to(cum_t[h0:h0 + 1, :], (q, q))
                row1 = jnp.broadcast_to(cum_t[h1:h1 + 1, :], (q, q))
                dec0 = jnp.exp2(jnp.where(causal, col0 - row0, -jnp.inf))
                dec1 = jnp.exp2(jnp.where(causal, col1 - row1, -jnp.inf))
                m_pair = jnp.concatenate([(cb * dec0).astype(BF16), (cb * dec1).astype(BF16)], axis=1)
                col_p = jnp.where(lane_lo, col0, col1)
                dt_p = jnp.where(lane_lo,
                                 jnp.broadcast_to(dtc[:, h0:h0 + 1], (q, LANES)),
                                 jnp.broadcast_to(dtc[:, h1:h1 + 1], (q, LANES)))
                cols = slice(j * LANES, (j + 1) * LANES)
                xp = xact_scr[rows, cols]
                xdt = xp * dt_p
                x_bd = jnp.concatenate([jnp.where(lane_lo, xdt, 0.0).astype(BF16),
                                        jnp.where(lane_lo, 0.0, xdt).astype(BF16)], axis=0)
                y_p = _dot(m_pair, x_bd)
                y_p = y_p + y_inter[:, jj * LANES:(jj + 1) * LANES] * jnp.exp2(col_p)
                y_p = y_p + dexp_ref[:, cols] * xp
                yb_scr[rows, cols] = y_p
                last_p = col_p[q - 1:q, :]
                xt_parts.append((xdt * jnp.exp2(last_p - col_p)).astype(BF16))
                el_parts.append(jnp.exp2(last_p))
            xt = jnp.concatenate(xt_parts, axis=1)
            el = jnp.concatenate(el_parts, axis=1)
            st_scr[:, g * gw:(g + 1) * gw] = st_g * el + _dot(bg_t, xt)

    @pl.when(t == nt - 1)
    def _():
        for k in range(D_INNER // LANES):
            hst_ref[0, k * LANES:(k + 1) * LANES, :] = st_scr[:, k * LANES:(k + 1) * LANES].T

    for g in range(SSD_GROUPS):
        cols = slice(g * gw, (g + 1) * gw)
        z = _dot(u_scr[...], wz_ref[:, cols])
        yg = yb_scr[:, cols] * _silu(z)
        yg = _rms(yg, normw_ref[:, cols])
        yb_scr[:, cols] = yg
    out = _dot(yb_scr[...].astype(BF16), wout_ref[...])
    _store_residual(y_ref, x_ref, xs_scr, _rms(out, postw_ref[...]), tb, q)


def _ssd_prompt(x, prew, win, wdt, convw, convb, dtb, alog, dexp, normw, wout, postw):
    bsz, seq, _ = x.shape
    tb = L0_TB
    grid = (bsz, seq // tb)
    consts = (prew, win, wdt, convw, convb, dtb, alog, dexp, normw, wout, postw)
    in_specs = [pl.BlockSpec((1, tb, D_MODEL), lambda b, t: (b, t, 0))]
    in_specs += [_const_spec(c.shape) for c in consts]
    out_shape = (jax.ShapeDtypeStruct((bsz, seq, D_MODEL), F32),
                 jax.ShapeDtypeStruct((bsz, CONV_WIDTH - 1, SSD_CONV_DIM), F32),
                 jax.ShapeDtypeStruct((bsz, D_INNER, SSD_STATE), F32))
    out_specs = (pl.BlockSpec((1, tb, D_MODEL), lambda b, t: (b, t, 0)),
                 pl.BlockSpec((1, CONV_WIDTH - 1, SSD_CONV_DIM), lambda b, t: (b, 0, 0)),
                 pl.BlockSpec((1, D_INNER, SSD_STATE), lambda b, t: (b, 0, 0)))
    scratch = [pltpu.VMEM((D_MODEL // LANES, tb, LANES), F32),
               pltpu.VMEM((tb, D_MODEL), BF16),
               pltpu.VMEM((SSD_CONV_DIM // CONV_TILE, WRAP, CONV_TILE), F32),
               pltpu.VMEM((tb, D_INNER), F32),
               pltpu.VMEM((tb, SSD_GN), F32),
               pltpu.VMEM((tb, SSD_GN), F32),
               pltpu.VMEM((tb, LANES), F32),
               pltpu.VMEM((tb, LANES), F32),
               pltpu.VMEM((tb, D_INNER), F32),
               pltpu.VMEM((SSD_STATE, D_INNER), F32)]
    scratch += [pltpu.VMEM((tb // SSD_CHUNK, WRAP + SSD_CHUNK, CONV_TILE), F32)
                for _ in range(SSD_CONV_DIM // CONV_TILE)]
    return pl.pallas_call(
        _ssd_prompt_kernel, grid=grid, in_specs=in_specs, out_specs=out_specs,
        out_shape=out_shape, scratch_shapes=scratch, name="ssd_prompt",
        compiler_params=pltpu.CompilerParams(
            dimension_semantics=("arbitrary", "arbitrary"), vmem_limit_bytes=VMEM_LIMIT),
    )(x, *consts)


def _pad_lanes(a, n=LANES):
    return jnp.pad(a, ((0, 0), (0, n - a.shape[-1])))


L1_TB = 256


def _log_sigmoid(x):
    return jnp.minimum(x, 0.0) - jnp.log1p(jnp.exp(-jnp.abs(x)))


def _dot_nt(a, b):
    return lax.dot_general(a, b, (((1,), (1,)), ((), ())), preferred_element_type=F32)


def _dot_tn(a, b):
    return lax.dot_general(a, b, (((0,), (0,)), ((), ())), preferred_element_type=F32)


def _mlstm_prompt_kernel(x_ref, prew_ref, win_ref, convw_ref, convb_ref,
                         wq_ref, wk_ref, wv_ref, wg_ref, bg_ref, hnw_ref, skip_ref, wout_ref,
                         postw_ref,
                         y_ref, convst_ref, c_ref, n_ref, m_ref,
                         u_scr, xe_scr, xc_scr, qkv_scr, qs_scr, h_scr, hb_scr, n_scr, m_scr):
    tb = L1_TB
    hd_dim = MLSTM_HEAD_DIM
    t = pl.program_id(1)
    nt = pl.num_programs(1)
    wxm_ref, wz_ref, wo_ref = _mlstm_win_views(win_ref)

    @pl.when(t == 0)
    def _():
        xe_scr[0:HALO, :] = jnp.zeros((HALO, D_INNER), F32)
        c_ref[...] = jnp.zeros_like(c_ref)
        n_scr[...] = jnp.zeros_like(n_scr)
        m_scr[...] = jnp.zeros_like(m_scr)

    x = x_ref[0]
    u = _rms(x, prew_ref[...]).astype(BF16)
    u_scr[...] = u

    nt_cols = 512
    for n0 in range(0, D_INNER, nt_cols):
        xe_scr[HALO:HALO + tb, n0:n0 + nt_cols] = _dot(u, wxm_ref[:, n0:n0 + nt_cols])

    for n0 in range(0, D_INNER, nt_cols):
        acc = convb_ref[:, n0:n0 + nt_cols]
        for k in range(CONV_WIDTH):
            off = HALO - (CONV_WIDTH - 1) + k
            acc = acc + xe_scr[off:off + tb, n0:n0 + nt_cols] * convw_ref[k:k + 1, n0:n0 + nt_cols]
        xc_scr[:, n0:n0 + nt_cols] = _silu(acc)

    scale = hd_dim ** -0.5
    for hd in range(MLSTM_HEADS):
        cols = slice(hd * hd_dim, (hd + 1) * hd_dim)
        xc_h = xc_scr[:, cols].astype(BF16)
        xm_h = xe_scr[HALO:HALO + tb, cols].astype(BF16)
        q_h = _dot(xc_h, wq_ref[hd])
        qkv_scr[:, cols] = q_h.astype(BF16)
        qs_scr[:, cols] = (q_h * scale).astype(BF16)
        qkv_scr[:, D_INNER + hd * hd_dim:D_INNER + (hd + 1) * hd_dim] = _dot(xc_h, wk_ref[hd]).astype(BF16)
        qkv_scr[:, 2 * D_INNER + hd * hd_dim:2 * D_INNER + (hd + 1) * hd_dim] = _dot(xm_h, wv_ref[hd]).astype(BF16)

    @pl.when(t == nt - 1)
    def _():
        convst_ref[0] = xe_scr[HALO + tb - (CONV_WIDTH - 1):HALO + tb, :]

    xe_scr[0:HALO, :] = xe_scr[tb:tb + HALO, :]

    gates = _dot(qkv_scr[...], wg_ref[...]) + bg_ref[...]
    li_all = gates
    lf_all = pltpu.roll(_log_sigmoid(gates), LANES - MLSTM_HEADS, axis=1)
    causal = _tri(tb)
    bcum = _dot_exact(causal.astype(F32), lf_all)
    m_row = m_scr[0:1, :]
    inter_all = bcum + m_row
    bcum_t = bcum.T
    li_t = li_all.T
    lane_row = lax.broadcasted_iota(jnp.int32, (1, LANES), 1)
    m_new_row = m_row

    for hd in range(MLSTM_HEADS):
        cols = slice(hd * hd_dim, (hd + 1) * hd_dim)
        kcols = slice(D_INNER + hd * hd_dim, D_INNER + (hd + 1) * hd_dim)
        vcols = slice(2 * D_INNER + hd * hd_dim, 2 * D_INNER + (hd + 1) * hd_dim)
        b_col = bcum[:, hd:hd + 1]
        dmat = b_col + (li_t[hd:hd + 1, :] - bcum_t[hd:hd + 1, :])
        dmat = jnp.where(causal, dmat, -jnp.inf)
        inter = inter_all[:, hd:hd + 1]
        m_t = jnp.maximum(inter, jnp.max(dmat, axis=1, keepdims=True))
        w = jnp.exp(dmat - m_t)
        qs_h = qs_scr[:, cols]
        k_h = qkv_scr[:, kcols]
        v_h = qkv_scr[:, vcols]
        s = _dot_nt(qs_h, k_h) * w
        inter_w = jnp.exp(inter - m_t)
        c_old = c_ref[0, hd]
        n_row = n_scr[0:1, cols]
        num = _dot(s.astype(BF16), v_h) + inter_w * _dot(qs_h, c_old.astype(BF16))
        qn = jnp.sum(qs_h.astype(F32) * n_row, axis=1, keepdims=True)
        den = jnp.sum(s, axis=1, keepdims=True) + inter_w * qn
        hh = num / jnp.maximum(jnp.abs(den), jnp.exp(-m_t))
        hc = hh - jnp.mean(hh, axis=1, keepdims=True)
        h_scr[:, cols] = hc * lax.rsqrt(jnp.mean(hc * hc, axis=1, keepdims=True) + EPS)
        m_new = m_t[tb - 1:tb, :]
        b_last = b_col[tb - 1:tb, :]
        wts = jnp.exp(b_last - b_col + li_all[:, hd:hd + 1] - m_new)
        cd = jnp.exp(b_last + m_row[:, hd:hd + 1] - m_new)
        kw = k_h.astype(F32) * wts
        c_ref[0, hd] = cd * c_old + _dot_tn(kw.astype(BF16), v_h)
        n_scr[0:1, cols] = cd * n_row + jnp.sum(kw, axis=0, keepdims=True)
        m_new_row = jnp.where(lane_row == hd, m_new, m_new_row)

    m_scr[0:1, :] = m_new_row

    @pl.when(t == nt - 1)
    def _():
        n_ref[0] = n_scr[0:1, :]
        m_ref[0] = m_new_row

    for n0 in range(0, D_INNER, nt_cols):
        cols = slice(n0, n0 + nt_cols)
        o = _sigmoid(_dot(u_scr[...], wo_ref[:, cols]))
        z = _dot(u_scr[...], wz_ref[:, cols])
        hg = o * (h_scr[:, cols] * hnw_ref[:, cols]) + skip_ref[:, cols] * xc_scr[:, cols]
        hb_scr[:, cols] = (hg * _silu(z)).astype(BF16)
    out = _dot(hb_scr[...], wout_ref[...])
    y_ref[0] = x_ref[0] + _rms(out, postw_ref[...])


def _mlstm_prompt(x, prew, win, convw, convb, wq, wk, wv, wg, bg, hnw, skip, wout, postw):
    bsz, seq, _ = x.shape
    tb = L1_TB
    grid = (bsz, seq // tb)
    consts = (prew, win, convw, convb, wq, wk, wv, wg, bg, hnw, skip, wout, postw)
    in_specs = [pl.BlockSpec((1, tb, D_MODEL), lambda b, t: (b, t, 0))]
    in_specs += [_const_spec(c.shape) for c in consts]
    hd_dim = MLSTM_HEAD_DIM
    out_shape = (jax.ShapeDtypeStruct((bsz, seq, D_MODEL), F32),
                 jax.ShapeDtypeStruct((bsz, CONV_WIDTH - 1, D_INNER), F32),
                 jax.ShapeDtypeStruct((bsz, MLSTM_HEADS, hd_dim, hd_dim), F32),
                 jax.ShapeDtypeStruct((bsz, 1, D_INNER), F32),
                 jax.ShapeDtypeStruct((bsz, 1, LANES), F32))
    out_specs = (pl.BlockSpec((1, tb, D_MODEL), lambda b, t: (b, t, 0)),
                 pl.BlockSpec((1, CONV_WIDTH - 1, D_INNER), lambda b, t: (b, 0, 0)),
                 pl.BlockSpec((1, MLSTM_HEADS, hd_dim, hd_dim), lambda b, t: (b, 0, 0, 0)),
                 pl.BlockSpec((1, 1, D_INNER), lambda b, t: (b, 0, 0)),
                 pl.BlockSpec((1, 1, LANES), lambda b, t: (b, 0, 0)))
    scratch = [pltpu.VMEM((tb, D_MODEL), BF16),
               pltpu.VMEM((tb + HALO, D_INNER), F32),
               pltpu.VMEM((tb, D_INNER), F32),
               pltpu.VMEM((tb, 3 * D_INNER), BF16),
               pltpu.VMEM((tb, D_INNER), BF16),
               pltpu.VMEM((tb, D_INNER), F32),
               pltpu.VMEM((tb, D_INNER), BF16),
               pltpu.VMEM((SUBLANES, D_INNER), F32),
               pltpu.VMEM((SUBLANES, LANES), F32)]
    return pl.pallas_call(
        _mlstm_prompt_kernel, grid=grid, in_specs=in_specs, out_specs=out_specs,
        out_shape=out_shape, scratch_shapes=scratch, name="mlstm_prompt",
        compiler_params=pltpu.CompilerParams(
            dimension_semantics=("arbitrary", "arbitrary"), vmem_limit_bytes=VMEM_LIMIT),
    )(x, *consts)


N_SAMPLE = 128
S0_SEQS = 4


def _split3(v):
    hi = v.astype(BF16)
    r1 = v - hi.astype(F32)
    mid = r1.astype(BF16)
    lo = (r1 - mid.astype(F32)).astype(BF16)
    return hi, mid, lo


def _expand_heads(v, width):
    rr = lax.broadcasted_iota(jnp.int32, (3 * LANES, D_INNER), 0) & (LANES - 1)
    cc = lax.broadcasted_iota(jnp.int32, (3 * LANES, D_INNER), 1) // width
    e3 = jnp.where(rr == cc, 1.0, 0.0).astype(BF16)
    return _dot(jnp.concatenate(_split3(v), axis=1), e3)


def _ssd_sample_pre_kernel(x_ref, cs_ref, prew_ref, win_ref, wdt_ref, convw_ref, convb_ref,
                           dtb_ref, alog_ref,
                           z_ref, xact_ref, b_ref, ct_ref, xdtt_ref, dec_ref, csnew_ref):
    cd = SSD_CONV_DIM
    wz_ref, wxbc_ref = _ssd_win_views(win_ref)
    u = _rms(x_ref[...], prew_ref[...]).astype(BF16)
    xbc = _dot(u, wxbc_ref[...])
    acc = convb_ref[...] + xbc * convw_ref[CONV_WIDTH - 1:CONV_WIDTH, :]
    for k in range(CONV_WIDTH - 1):
        acc = acc + cs_ref[:, k * cd:(k + 1) * cd] * convw_ref[k:k + 1, :]
    act = _silu(acc)
    csnew_ref[:, 0:2 * cd] = cs_ref[:, cd:3 * cd]
    csnew_ref[:, 2 * cd:3 * cd] = xbc
    xact = act[:, :D_INNER]
    dt = _softplus(_dot(u, wdt_ref[...]) + dtb_ref[...])
    dec_ref[...] = jnp.exp(dt * (-jnp.exp(alog_ref[...])))
    xdt = xact * _expand_heads(dt, SSD_HEAD_DIM)
    xdtt_ref[...] = xdt.T.astype(BF16)
    z_ref[...] = _dot(u, wz_ref[...])
    xact_ref[...] = xact
    b_ref[...] = act[:, D_INNER:D_INNER + SSD_GN]
    ct_ref[...] = act[:, D_INNER + SSD_GN:].T


def _ssd_sample_pre(x, cs, prew, win, wdt, convw, convb, dtb, alog):
    n = x.shape[0]
    out_shape = (jax.ShapeDtypeStruct((n, D_INNER), F32),
                 jax.ShapeDtypeStruct((n, D_INNER), F32),
                 jax.ShapeDtypeStruct((n, SSD_GN), F32),
                 jax.ShapeDtypeStruct((SSD_GN, n), F32),
                 jax.ShapeDtypeStruct((D_INNER, n), BF16),
                 jax.ShapeDtypeStruct((n, LANES), F32),
                 jax.ShapeDtypeStruct((n, (CONV_WIDTH - 1) * SSD_CONV_DIM), F32))
    return pl.pallas_call(
        _ssd_sample_pre_kernel, out_shape=out_shape, name="ssd_sample_pre",
        compiler_params=pltpu.CompilerParams(vmem_limit_bytes=VMEM_LIMIT),
    )(x, cs, prew, win, wdt, convw, convb, dtb, alog)


def _ssd_sample_state_kernel(dec_ref, h_ref, xdtt_ref, b_ref, ct_ref, hnew_ref, yt_ref):
    j = pl.program_id(0)
    gw = D_INNER // SSD_GROUPS
    hpg = SSD_HEADS // SSD_GROUPS

    @pl.when(j == 0)
    def _():
        yt_ref[...] = jnp.zeros_like(yt_ref)

    rowid = lax.broadcasted_iota(jnp.int32, (N_SAMPLE, SSD_STATE), 0)
    colid = lax.broadcasted_iota(jnp.int32, (SSD_STATE, N_SAMPLE), 1)
    for pp in range(S0_SEQS // 2):
        seqs = (j * S0_SEQS + 2 * pp, j * S0_SEQS + 2 * pp + 1)
        for g in range(SSD_GROUPS):
            bg = b_ref[:, g * SSD_STATE:(g + 1) * SSD_STATE]
            ctg = ct_ref[g * SSD_STATE:(g + 1) * SSD_STATE, :]
            rb = jnp.concatenate([jnp.where(rowid == s, bg, 0.0) for s in seqs], axis=1).astype(BF16)
            upd = _dot(xdtt_ref[g * gw:(g + 1) * gw, :], rb)
            for i, s in enumerate(seqs):
                for r in range(hpg):
                    rows = slice(g * gw + r * SSD_HEAD_DIM, g * gw + (r + 1) * SSD_HEAD_DIM)
                    dec = dec_ref[s * SSD_HEADS + g * hpg + r]
                    hn = (h_ref[2 * pp + i, rows, :] * dec
                          + upd[r * SSD_HEAD_DIM:(r + 1) * SSD_HEAD_DIM, i * SSD_STATE:(i + 1) * SSD_STATE])
                    hnew_ref[2 * pp + i, rows, :] = hn
            hn_pair = jnp.concatenate(
                [hnew_ref[2 * pp + i, g * gw:(g + 1) * gw, :] for i in range(2)], axis=1).astype(BF16)
            wc = jnp.concatenate([jnp.where(colid == s, ctg, 0.0) for s in seqs], axis=0).astype(BF16)
            yt_ref[g * gw:(g + 1) * gw, :] += _dot(hn_pair, wc)


def _ssd_sample_state(dec_flat, h, xdtt, b, ct):
    n = h.shape[0]
    grid = (n // S0_SEQS,)
    in_specs = [pl.BlockSpec(memory_space=pltpu.SMEM),
                pl.BlockSpec((S0_SEQS, D_INNER, SSD_STATE), lambda j: (j, 0, 0)),
                _const_spec(xdtt.shape), _const_spec(b.shape), _const_spec(ct.shape)]
    out_shape = (jax.ShapeDtypeStruct(h.shape, F32),
                 jax.ShapeDtypeStruct((D_INNER, n), F32))
    out_specs = (pl.BlockSpec((S0_SEQS, D_INNER, SSD_STATE), lambda j: (j, 0, 0)),
                 pl.BlockSpec((D_INNER, n), lambda j: (0, 0)))
    return pl.pallas_call(
        _ssd_sample_state_kernel, grid=grid, in_specs=in_specs, out_specs=out_specs,
        out_shape=out_shape, name="ssd_sample_state",
        compiler_params=pltpu.CompilerParams(
            dimension_semantics=("arbitrary",), vmem_limit_bytes=VMEM_LIMIT),
    )(dec_flat, h, xdtt, b, ct)


def _head_cols(a, hd):
    return a[:, hd * MLSTM_HEAD_DIM:(hd + 1) * MLSTM_HEAD_DIM]


def _sample_mid_kernel(x_ref, yt_ref, xact_ref, z_ref, dexp_ref, normw_ref, wout0_ref, postw0_ref,
                       mcs_ref, n0_ref, m0_ref, prew1_ref, win1_ref, convw_ref,
                       convb_ref, wq_ref, wk_ref, wv_ref, wg_ref, bg_ref,
                       x1_ref, z1_ref, o1_ref, xc_ref, v_ref, kt_ref, wv_out_ref, qs_ref,
                       s_ref, iw_ref, hden_ref, mcsnew_ref, nnew_ref, mnew_ref):
    gw = D_INNER // SSD_GROUPS
    wxm_ref, wz1_ref, wo1_ref = _mlstm_win_views(win1_ref)
    y = yt_ref[...].T + dexp_ref[...] * xact_ref[...]
    y = y * _silu(z_ref[...])
    parts = []
    for g in range(SSD_GROUPS):
        cols = slice(g * gw, (g + 1) * gw)
        parts.append(_rms(y[:, cols], normw_ref[:, cols]).astype(BF16))
    out0 = _dot(jnp.concatenate(parts, axis=1), wout0_ref[...])
    x1 = x_ref[...] + _rms(out0, postw0_ref[...])
    x1_ref[...] = x1

    u = _rms(x1, prew1_ref[...]).astype(BF16)
    xm = _dot(u, wxm_ref[...])
    z1_ref[...] = _dot(u, wz1_ref[...])
    o1_ref[...] = _dot(u, wo1_ref[...])
    acc = convb_ref[...] + xm * convw_ref[CONV_WIDTH - 1:CONV_WIDTH, :]
    for k in range(CONV_WIDTH - 1):
        acc = acc + mcs_ref[:, k * D_INNER:(k + 1) * D_INNER] * convw_ref[k:k + 1, :]
    xc = _silu(acc)
    xc_ref[...] = xc
    mcsnew_ref[:, 0:2 * D_INNER] = mcs_ref[:, D_INNER:3 * D_INNER]
    mcsnew_ref[:, 2 * D_INNER:3 * D_INNER] = xm

    scale = MLSTM_HEAD_DIM ** -0.5
    qs, ks, vs = [], [], []
    for hd in range(MLSTM_HEADS):
        xc_h = _head_cols(xc, hd).astype(BF16)
        xm_h = _head_cols(xm, hd).astype(BF16)
        qs.append(_dot(xc_h, wq_ref[hd]))
        ks.append(_dot(xc_h, wk_ref[hd]))
        vs.append(_dot(xm_h, wv_ref[hd]))
    qkv = jnp.concatenate(qs + ks + vs, axis=1).astype(BF16)
    gates = _dot(qkv, wg_ref[...]) + bg_ref[...]
    li = gates
    lf = pltpu.roll(_log_sigmoid(gates), LANES - MLSTM_HEADS, axis=1)
    m0 = m0_ref[...]
    inter = lf + m0
    m_t = jnp.maximum(inter, li)
    w = jnp.exp(li - m_t)
    iw = jnp.exp(inter - m_t)
    mnew_ref[...] = m_t

    lane = lax.broadcasted_iota(jnp.int32, (N_SAMPLE, LANES), 1)
    qk_arr = jnp.zeros((N_SAMPLE, LANES), F32)
    qn_arr = jnp.zeros((N_SAMPLE, LANES), F32)
    kb_parts = []
    for hd in range(MLSTM_HEADS):
        cols = slice(hd * MLSTM_HEAD_DIM, (hd + 1) * MLSTM_HEAD_DIM)
        q_h = (qs[hd] * scale).astype(BF16).astype(F32)
        k_h = ks[hd].astype(BF16).astype(F32)
        n_h = n0_ref[:, cols]
        qk_arr = jnp.where(lane == hd, jnp.sum(q_h * k_h, axis=1, keepdims=True), qk_arr)
        qn_arr = jnp.where(lane == hd, jnp.sum(q_h * n_h, axis=1, keepdims=True), qn_arr)
        w_h = w[:, hd:hd + 1]
        iw_h = iw[:, hd:hd + 1]
        nnew_ref[:, cols] = iw_h * n_h + w_h * k_h
        wv_out_ref[:, cols] = w_h * vs[hd]
        v_ref[:, cols] = vs[hd]
        qs_ref[:, cols] = q_h
        kb_parts.append(k_h)
    kt_ref[...] = jnp.concatenate(kb_parts, axis=1).T.astype(BF16)
    s = qk_arr * w
    den = s + iw * qn_arr
    s_ref[...] = s
    iw_ref[...] = iw
    hden_ref[...] = jnp.maximum(jnp.abs(den), jnp.exp(-m_t))


def _sample_mid(*args):
    n = N_SAMPLE
    wide = jax.ShapeDtypeStruct((n, D_INNER), F32)
    small = jax.ShapeDtypeStruct((n, LANES), F32)
    out_shape = (jax.ShapeDtypeStruct((n, D_MODEL), F32),
                 wide, wide, wide, wide,
                 jax.ShapeDtypeStruct((D_INNER, n), BF16),
                 wide, wide,
                 small, small, small,
                 jax.ShapeDtypeStruct((n, (CONV_WIDTH - 1) * D_INNER), F32),
                 wide, small)
    return pl.pallas_call(
        _sample_mid_kernel, out_shape=out_shape, name="sample_mid",
        compiler_params=pltpu.CompilerParams(vmem_limit_bytes=VMEM_LIMIT),
    )(*args)


def _mlstm_sample_state_kernel(cd_ref, c_ref, kt_ref, wv_ref, qs_ref, cnew_ref, num_ref):
    b = pl.program_id(0)
    rowid = lax.broadcasted_iota(jnp.int32, (N_SAMPLE, MLSTM_HEAD_DIM), 0)
    sub = lax.broadcasted_iota(jnp.int32, (SUBLANES, MLSTM_HEAD_DIM), 0)
    b8 = pl.multiple_of((b // SUBLANES) * SUBLANES, SUBLANES)
    for hd in range(MLSTM_HEADS):
        cols = slice(hd * MLSTM_HEAD_DIM, (hd + 1) * MLSTM_HEAD_DIM)
        c_old = c_ref[0, hd]
        rv = jnp.where(rowid == b, wv_ref[:, cols], 0.0).astype(BF16)
        upd = _dot(kt_ref[cols, :], rv)
        cnew_ref[0, hd] = cd_ref[b * MLSTM_HEADS + hd] * c_old + upd
        q8 = qs_ref[pl.ds(b8, SUBLANES), cols].astype(BF16)
        r8 = _dot(q8, c_old.astype(BF16))
        num_ref[pl.ds(b, 1), cols] = jnp.sum(jnp.where(sub == b - b8, r8, 0.0), axis=0, keepdims=True)


def _mlstm_sample_state(cd_flat, c, kt, wv, qs):
    n = c.shape[0]
    blk = (1, MLSTM_HEADS, MLSTM_HEAD_DIM, MLSTM_HEAD_DIM)
    in_specs = [pl.BlockSpec(memory_space=pltpu.SMEM),
                pl.BlockSpec(blk, lambda b: (b, 0, 0, 0)),
                _const_spec(kt.shape), _const_spec(wv.shape), _const_spec(qs.shape)]
    out_shape = (jax.ShapeDtypeStruct(c.shape, F32), jax.ShapeDtypeStruct((n, D_INNER), F32))
    out_specs = (pl.BlockSpec(blk, lambda b: (b, 0, 0, 0)),
                 pl.BlockSpec((n, D_INNER), lambda b: (0, 0)))
    return pl.pallas_call(
        _mlstm_sample_state_kernel, grid=(n,), in_specs=in_specs, out_specs=out_specs,
        out_shape=out_shape, name="mlstm_sample_state",
        compiler_params=pltpu.CompilerParams(
            dimension_semantics=("arbitrary",), vmem_limit_bytes=VMEM_LIMIT),
    )(cd_flat, c, kt, wv, qs)


def _mlstm_sample_post_kernel(x1_ref, num_ref, v_ref, s_ref, iw_ref, hden_ref, o1_ref, z1_ref,
                              xc_ref, hnw_ref, skip_ref, wout_ref, postw_ref, y_ref):
    parts = []
    for hd in range(MLSTM_HEADS):
        cols = slice(hd * MLSTM_HEAD_DIM, (hd + 1) * MLSTM_HEAD_DIM)
        num = s_ref[:, hd:hd + 1] * v_ref[:, cols] + iw_ref[:, hd:hd + 1] * num_ref[:, cols]
        hh = num / hden_ref[:, hd:hd + 1]
        hc = hh - jnp.mean(hh, axis=1, keepdims=True)
        hn = hc * lax.rsqrt(jnp.mean(hc * hc, axis=1, keepdims=True) + EPS)
        hg = _sigmoid(o1_ref[:, cols]) * (hn * hnw_ref[:, cols]) + skip_ref[:, cols] * xc_ref[:, cols]
        parts.append((hg * _silu(z1_ref[:, cols])).astype(BF16))
    out = _dot(jnp.concatenate(parts, axis=1), wout_ref[...])
    y_ref[...] = x1_ref[...] + _rms(out, postw_ref[...])


def _mlstm_sample_post(*args):
    return pl.pallas_call(
        _mlstm_sample_post_kernel, out_shape=jax.ShapeDtypeStruct((N_SAMPLE, D_MODEL), F32),
        name="mlstm_sample_post",
        compiler_params=pltpu.CompilerParams(vmem_limit_bytes=VMEM_LIMIT),
    )(*args)


def kernel(x_prompt, x_sample, state_ssd_conv, state_ssd, state_mlstm_conv, state_mlstm_c, state_mlstm_n, state_mlstm_m, pre_norm_w, post_norm_w, ssd_w_in, ssd_conv_w, ssd_conv_b, ssd_dt_bias, ssd_a_log, ssd_d, ssd_norm_w, ssd_w_out, ml_w_in, ml_conv_w, ml_conv_b, ml_w_q, ml_w_k, ml_w_v, ml_w_gate, ml_b_gate, ml_head_norm_w, ml_skip, ml_w_out):
    win0 = ssd_w_in[0].astype(BF16)
    wdt = _pad_lanes(win0[:, D_INNER + SSD_CONV_DIM:])
    dtb = _pad_lanes(ssd_dt_bias[0][None])
    alog = _pad_lanes(ssd_a_log[0][None])
    dexp = jnp.repeat(ssd_d[0], SSD_HEAD_DIM)[None]
    prew0, postw0 = pre_norm_w[0][None], post_norm_w[0][None]
    prew1, postw1 = pre_norm_w[1][None], post_norm_w[1][None]
    convw0, convb0 = ssd_conv_w[0], ssd_conv_b[0][None]
    normw0 = ssd_norm_w[0][None]
    wout0 = ssd_w_out[0].astype(BF16)
    win1 = ml_w_in[0].astype(BF16)
    wg = _pad_lanes(ml_w_gate[0]).astype(BF16)
    bg = _pad_lanes(ml_b_gate[0][None])
    convw1, convb1 = ml_conv_w[0], ml_conv_b[0][None]
    wq, wk, wv = ml_w_q[0].astype(BF16), ml_w_k[0].astype(BF16), ml_w_v[0].astype(BF16)
    hnw, skip = ml_head_norm_w[0][None], ml_skip[0][None]
    wout1 = ml_w_out[0].astype(BF16)

    bp = x_prompt.shape[0]
    y0, p_sc, p_sh = _ssd_prompt(x_prompt, prew0, win0, wdt, convw0, convb0, dtb, alog, dexp,
                                 normw0, wout0, postw0)
    y_prompt, p_mc, p_c, p_n, p_m = _mlstm_prompt(
        y0, prew1, win1, convw1, convb1, wq, wk, wv, wg, bg, hnw, skip, wout1, postw1)
    p_sc = p_sc[None]
    p_sh = p_sh.reshape(1, bp, SSD_HEADS, SSD_HEAD_DIM, SSD_STATE)
    p_mc = p_mc[None]
    p_c = p_c[None]
    p_n = p_n.reshape(1, bp, MLSTM_HEADS, MLSTM_HEAD_DIM)
    p_m = p_m[:, 0, :MLSTM_HEADS][None]

    n = x_sample.shape[0]
    xs = x_sample.reshape(n, D_MODEL)
    cs0 = state_ssd_conv[0].reshape(n, (CONV_WIDTH - 1) * SSD_CONV_DIM)
    z0, xact0, b0, ct0, xdtt0, dec0, csnew0 = _ssd_sample_pre(
        xs, cs0, prew0, win0, wdt, convw0, convb0, dtb, alog)
    h0 = state_ssd[0].reshape(n, D_INNER, SSD_STATE)
    hnew0, yt0 = _ssd_sample_state(dec0[:, :SSD_HEADS].reshape(-1), h0, xdtt0, b0, ct0)
    mcs = state_mlstm_conv[0].reshape(n, (CONV_WIDTH - 1) * D_INNER)
    n0 = state_mlstm_n[0].reshape(n, D_INNER)
    m0 = _pad_lanes(state_mlstm_m[0])
    (x1, z1, o1, xc1, v1, kt1, wv1, qs1, s1, iw1, hden1, mcsnew, nnew, mnew) = _sample_mid(
        xs, yt0, xact0, z0, dexp, normw0, wout0, postw0,
        mcs, n0, m0, prew1, win1, convw1, convb1, wq, wk, wv, wg, bg)
    cnew, num1 = _mlstm_sample_state(iw1[:, :MLSTM_HEADS].reshape(-1), state_mlstm_c[0], kt1, wv1, qs1)
    y_s = _mlstm_sample_post(x1, num1, v1, s1, iw1, hden1, o1, z1, xc1, hnw, skip, wout1, postw1)

    y_sample = y_s.reshape(n, 1, D_MODEL)
    s_sc = csnew0.reshape(1, n, CONV_WIDTH - 1, SSD_CONV_DIM)
    s_sh = hnew0.reshape(1, n, SSD_HEADS, SSD_HEAD_DIM, SSD_STATE)
    s_mc = mcsnew.reshape(1, n, CONV_WIDTH - 1, D_INNER)
    s_c = cnew[None]
    s_n = nnew.reshape(1, n, MLSTM_HEADS, MLSTM_HEAD_DIM)
    s_m = mnew[:, :MLSTM_HEADS][None]
    return (y_prompt, y_sample, p_sc, s_sc, p_sh, s_sh, p_mc, s_mc, p_c, s_c, p_n, s_n, p_m, s_m)
```

```python
import jax
import jax.numpy as jnp
from jax import lax
from jax.experimental import pallas as pl
from jax.experimental.pallas import tpu as pltpu

F32 = jnp.float32
BF16 = jnp.bfloat16

D_MODEL = 1024
D_INNER = 2048
CONV_WIDTH = 4
SSD_HEAD_DIM = 64
SSD_HEADS = 32
SSD_GROUPS = 4
SSD_STATE = 128
SSD_GN = SSD_GROUPS * SSD_STATE
SSD_CONV_DIM = D_INNER + 2 * SSD_GN
SSD_CHUNK = 128
MLSTM_HEADS = 4
MLSTM_HEAD_DIM = 512
EPS = 1e-6

LANES = 128
SUBLANES = 8
HALO = SUBLANES
VMEM_LIMIT = 58 * 1024 * 1024

L0_TB = 256


def _sigmoid(x):
    return 1.0 / (1.0 + jnp.exp(-x))


def _silu(x):
    return x * _sigmoid(x)


def _softplus(x):
    return jnp.maximum(x, 0.0) + jnp.log1p(jnp.exp(-jnp.abs(x)))


def _rms(x, w):
    ms = jnp.mean(x * x, axis=-1, keepdims=True)
    return x * lax.rsqrt(ms + EPS) * w


def _dot(a, b):
    return jnp.dot(a, b, preferred_element_type=F32)


def _dot_exact(a, b):
    return jnp.dot(a, b, preferred_element_type=F32, precision=lax.Precision.HIGHEST)


def _tri(n):
    r = lax.broadcasted_iota(jnp.int32, (n, n), 0)
    c = lax.broadcasted_iota(jnp.int32, (n, n), 1)
    return r >= c


WRAP = (CONV_WIDTH - 1) * SUBLANES
CONV_TILE = 512
LOG2E = 1.4426950408889634


def _stage_columns(x_ref, xs_scr):
    for k in range(xs_scr.shape[0]):
        xs_scr[k] = x_ref[0, :, k * LANES:(k + 1) * LANES]


def _load_permuted(xs_scr, tb, q):
    nv = q // SUBLANES
    cols = []
    for k in range(xs_scr.shape[0]):
        parts = [xs_scr[k, pl.ds(c * q + j, SUBLANES, stride=nv), :]
                 for c in range(tb // q) for j in range(nv)]
        cols.append(jnp.concatenate(parts, axis=0))
    return jnp.concatenate(cols, axis=1)


def _store_residual(y_ref, x_ref, xs_scr, branch, tb, q):
    nv = q // SUBLANES
    for k in range(xs_scr.shape[0]):
        for c in range(tb // q):
            for j in range(nv):
                r0 = c * q + j * SUBLANES
                xs_scr[k, pl.ds(c * q + j, SUBLANES, stride=nv), :] = branch[r0:r0 + SUBLANES,
                                                                             k * LANES:(k + 1) * LANES]
    for k in range(xs_scr.shape[0]):
        cols = slice(k * LANES, (k + 1) * LANES)
        y_ref[0, :, cols] = x_ref[0, :, cols] + xs_scr[k]


def _perm_causal(n, q):
    nv = q // SUBLANES
    r = lax.broadcasted_iota(jnp.int32, (n, n), 0)
    c = lax.broadcasted_iota(jnp.int32, (n, n), 1)
    tok = lambda p: (p & -q) + (p & (SUBLANES - 1)) * nv + ((p & (q - 1)) >> 3)
    return tok(r) >= tok(c)


def _fill_wrap_rows(xe_t, hist_t, nchunk, q):
    sub = lax.broadcasted_iota(jnp.int32, (SUBLANES, xe_t.shape[-1]), 0)
    for c in range(nchunk):
        for i in range(CONV_WIDTH - 1):
            src = slice(q + i * SUBLANES, q + (i + 1) * SUBLANES)
            cur = xe_t[c, src, :]
            prev = hist_t[i * SUBLANES:(i + 1) * SUBLANES, :] if c == 0 else xe_t[c - 1, src, :]
            xe_t[c, i * SUBLANES:(i + 1) * SUBLANES, :] = jnp.where(
                sub == 0, pltpu.roll(prev, 1, axis=0), pltpu.roll(cur, 1, axis=0))
    hist_t[...] = xe_t[nchunk - 1, q:q + WRAP, :]


def _conv_tile(xe_t, convw_ref, convb_ref, cols, c, q):
    acc = convb_ref[:, cols]
    for k in range(CONV_WIDTH):
        acc = acc + xe_t[c, k * SUBLANES:k * SUBLANES + q, :] * convw_ref[k:k + 1, cols]
    return acc


def _store_conv_state(convst_ref, xe_scr, nchunk, q, tile):
    for n in range(len(xe_scr)):
        for i in range(CONV_WIDTH - 1):
            row = q + (i + 1) * SUBLANES - 1
            convst_ref[0, i:i + 1, n * tile:(n + 1) * tile] = xe_scr[n][nchunk - 1, row:row + 1, :]


def _ssd_win_views(win_ref):
    return win_ref.at[:, 0:D_INNER], win_ref.at[:, D_INNER:D_INNER + SSD_CONV_DIM]


def _mlstm_win_views(win_ref):
    return (win_ref.at[:, 0:D_INNER], win_ref.at[:, D_INNER:2 * D_INNER],
            win_ref.at[:, 2 * D_INNER:3 * D_INNER])


def _const_spec(shape):
    nd = len(shape)
    return pl.BlockSpec(shape, lambda *_: (0,) * nd, pipeline_mode=pl.Buffered(1))


STREAM_SEQS = 2


def _stream_mlstm_state(step, nsteps, cd_ref, c_hbm, kt_ref, wv_ref, qs_ref, cnew_hbm, num_ref,
                        cin_scr, cout_scr, sem_in, sem_out):
    def in_copy(s):
        return pltpu.make_async_copy(c_hbm.at[pl.ds(s * STREAM_SEQS, STREAM_SEQS)], cin_scr, sem_in)

    def out_copy(s):
        return pltpu.make_async_copy(cout_scr, cnew_hbm.at[pl.ds(s * STREAM_SEQS, STREAM_SEQS)], sem_out)

    @pl.when(step == 0)
    def _():
        in_copy(0).start()

    in_copy(step).wait()

    @pl.when(step > 0)
    def _():
        out_copy(step - 1).wait()

    rowid = lax.broadcasted_iota(jnp.int32, (N_SAMPLE, MLSTM_HEAD_DIM), 0)
    sub = lax.broadcasted_iota(jnp.int32, (SUBLANES, MLSTM_HEAD_DIM), 0)
    for i in range(STREAM_SEQS):
        b = step * STREAM_SEQS + i
        b8 = pl.multiple_of((b // SUBLANES) * SUBLANES, SUBLANES)
        for hd in range(MLSTM_HEADS):
            cols = slice(hd * MLSTM_HEAD_DIM, (hd + 1) * MLSTM_HEAD_DIM)
            c_old = cin_scr[i, hd]
            rv = jnp.where(rowid == b, wv_ref[:, cols], 0.0).astype(BF16)
            upd = _dot(kt_ref[cols, :], rv)
            cout_scr[i, hd] = cd_ref[b * MLSTM_HEADS + hd] * c_old + upd
            q8 = qs_ref[pl.ds(b8, SUBLANES), cols].astype(BF16)
            r8 = _dot(q8, c_old.astype(BF16))
            num_ref[pl.ds(b, 1), cols] = jnp.sum(jnp.where(sub == b - b8, r8, 0.0), axis=0, keepdims=True)

    out_copy(step).start()

    @pl.when(step + 1 < nsteps)
    def _():
        in_copy(step + 1).start()

    @pl.when(step == nsteps - 1)
    def _():
        out_copy(step).wait()


def _ssd_prompt_kernel(cd_ref, x_ref, prew_ref, win_ref, wdt_ref, convw_ref, convb_ref,
                       dtb_ref, alog_ref, dexp_ref, normw_ref, wout_ref, postw_ref,
                       c_hbm, kt_ref, wv_ref, qs_ref,
                       y_ref, convst_ref, hst_ref, cnew_hbm, num_ref,
                       xs_scr, u_scr, hist_scr, xact_scr, b_scr, c_scr, dt_scr, da_scr, yb_scr,
                       st_scr, cin_scr, cout_scr, sem_in, sem_out, *xe_scr):
    tb = L0_TB
    t = pl.program_id(1)
    nt = pl.num_programs(1)
    wz_ref, wxbc_ref = _ssd_win_views(win_ref)

    _stream_mlstm_state(pl.program_id(0) * nt + t, pl.num_programs(0) * nt,
                        cd_ref, c_hbm, kt_ref, wv_ref, qs_ref, cnew_hbm, num_ref,
                        cin_scr, cout_scr, sem_in, sem_out)

    q = SSD_CHUNK
    nchunk = tb // q

    @pl.when(t == 0)
    def _():
        hist_scr[...] = jnp.zeros_like(hist_scr)
        st_scr[...] = jnp.zeros_like(st_scr)

    _stage_columns(x_ref, xs_scr)
    x = _load_permuted(xs_scr, tb, q)
    u = _rms(x, prew_ref[...]).astype(BF16)
    u_scr[...] = u

    for n in range(SSD_CONV_DIM // CONV_TILE):
        n0 = n * CONV_TILE
        cols = slice(n0, n0 + CONV_TILE)
        xe_t = xe_scr[n]
        res = _dot(u, wxbc_ref[:, cols])
        for c in range(nchunk):
            xe_t[c, WRAP:WRAP + q, :] = res[c * q:(c + 1) * q]
        _fill_wrap_rows(xe_t, hist_scr.at[n], nchunk, q)
        for c in range(nchunk):
            rows = slice(c * q, (c + 1) * q)
            act = _silu(_conv_tile(xe_t, convw_ref, convb_ref, cols, c, q))
            if n0 < D_INNER:
                xact_scr[rows, cols] = act
            elif n0 < D_INNER + SSD_GN:
                b_scr[rows, n0 - D_INNER:n0 - D_INNER + CONV_TILE] = act
            else:
                c_scr[rows, n0 - D_INNER - SSD_GN:n0 - D_INNER - SSD_GN + CONV_TILE] = act

    @pl.when(t == nt - 1)
    def _():
        _store_conv_state(convst_ref, xe_scr, nchunk, q, CONV_TILE)

    dt = _softplus(_dot(u, wdt_ref[...]) + dtb_ref[...])
    dt_scr[...] = dt
    da_scr[...] = dt * (-jnp.exp(alog_ref[...])) * LOG2E

    causal = _perm_causal(q, q)
    tri_f = causal.astype(F32)
    lane = lax.broadcasted_iota(jnp.int32, (q, LANES), 1)
    lane_lo = lane < SSD_HEAD_DIM
    hpg = SSD_HEADS // SSD_GROUPS
    gw = D_INNER // SSD_GROUPS

    for ci in range(nchunk):
        rows = slice(ci * q, (ci + 1) * q)
        cum = _dot_exact(tri_f, da_scr[rows, :])
        cum_t = cum.T
        dtc = dt_scr[rows, :]
        for g in range(SSD_GROUPS):
            bg = b_scr[rows, g * SSD_STATE:(g + 1) * SSD_STATE]
            cg = c_scr[rows, g * SSD_STATE:(g + 1) * SSD_STATE].astype(BF16)
            bg_t = bg.T.astype(BF16)
            cb = _dot(cg, bg_t)
            st_g = st_scr[:, g * gw:(g + 1) * gw]
            y_inter = _dot(cg, st_g.astype(BF16))
            xt_parts = []
            el_parts = []
            for jj in range(hpg // 2):
                j = g * (hpg // 2) + jj
                h0, h1 = 2 * j, 2 * j + 1
                col0 = jnp.broadcast_to(cum[:, h0:h0 + 1], (q, q))
                col1 = jnp.broadcast_to(cum[:, h1:h1 + 1], (q, q))
                row0 = jnp.broadcast_to(cum_t[h0:h0 + 1, :], (q, q))
                row1 = jnp.broadcast_to(cum_t[h1:h1 + 1, :], (q, q))
                dec0 = jnp.exp2(jnp.where(causal, col0 - row0, -jnp.inf))
                dec1 = jnp.exp2(jnp.where(causal, col1 - row1, -jnp.inf))
                m_pair = jnp.concatenate([(cb * dec0).astype(BF16), (cb * dec1).astype(BF16)], axis=1)
                col_p = jnp.where(lane_lo, col0, col1)
                dt_p = jnp.where(lane_lo,
                                 jnp.broadcast_to(dtc[:, h0:h0 + 1], (q, LANES)),
                                 jnp.broadcast_to(dtc[:, h1:h1 + 1], (q, LANES)))
                cols = slice(j * LANES, (j + 1) * LANES)
                xp = xact_scr[rows, cols]
                xdt = xp * dt_p
                x_bd = jnp.concatenate([jnp.where(lane_lo, xdt, 0.0).astype(BF16),
                                        jnp.where(lane_lo, 0.0, xdt).astype(BF16)], axis=0)
                y_p = _dot(m_pair, x_bd)
                y_p = y_p + y_inter[:, jj * LANES:(jj + 1) * LANES] * jnp.exp2(col_p)
                y_p = y_p + dexp_ref[:, cols] * xp
                yb_scr[rows, cols] = y_p
                last_p = col_p[q - 1:q, :]
                xt_parts.append((xdt * jnp.exp2(last_p - col_p)).astype(BF16))
                el_parts.append(jnp.exp2(last_p))
            xt = jnp.concatenate(xt_parts, axis=1)
            el = jnp.concatenate(el_parts, axis=1)
            st_scr[:, g * gw:(g + 1) * gw] = st_g * el + _dot(bg_t, xt)

    @pl.when(t == nt - 1)
    def _():
        for k in range(D_INNER // LANES):
            hst_ref[0, k * LANES:(k + 1) * LANES, :] = st_scr[:, k * LANES:(k + 1) * LANES].T

    for g in range(SSD_GROUPS):
        cols = slice(g * gw, (g + 1) * gw)
        z = _dot(u_scr[...], wz_ref[:, cols])
        yg = yb_scr[:, cols] * _silu(z)
        yg = _rms(yg, normw_ref[:, cols])
        yb_scr[:, cols] = yg
    out = _dot(yb_scr[...].astype(BF16), wout_ref[...])
    _store_residual(y_ref, x_ref, xs_scr, _rms(out, postw_ref[...]), tb, q)


def _ssd_prompt(x, prew, win, wdt, convw, convb, dtb, alog, dexp, normw, wout, postw,
                cd_flat, c_state, kt, wv, qs):
    bsz, seq, _ = x.shape
    tb = L0_TB
    grid = (bsz, seq // tb)
    n_sample = c_state.shape[0]
    assert n_sample == grid[0] * grid[1] * STREAM_SEQS
    consts = (prew, win, wdt, convw, convb, dtb, alog, dexp, normw, wout, postw)
    in_specs = [pl.BlockSpec(memory_space=pltpu.SMEM),
                pl.BlockSpec((1, tb, D_MODEL), lambda b, t: (b, t, 0))]
    in_specs += [_const_spec(c.shape) for c in consts]
    in_specs += [pl.BlockSpec(memory_space=pl.ANY),
                 _const_spec(kt.shape), _const_spec(wv.shape), _const_spec(qs.shape)]
    out_shape = (jax.ShapeDtypeStruct((bsz, seq, D_MODEL), F32),
                 jax.ShapeDtypeStruct((bsz, CONV_WIDTH - 1, SSD_CONV_DIM), F32),
                 jax.ShapeDtypeStruct((bsz, D_INNER, SSD_STATE), F32),
                 jax.ShapeDtypeStruct(c_state.shape, F32),
                 jax.ShapeDtypeStruct((n_sample, D_INNER), F32))
    out_specs = (pl.BlockSpec((1, tb, D_MODEL), lambda b, t: (b, t, 0)),
                 pl.BlockSpec((1, CONV_WIDTH - 1, SSD_CONV_DIM), lambda b, t: (b, 0, 0)),
                 pl.BlockSpec((1, D_INNER, SSD_STATE), lambda b, t: (b, 0, 0)),
                 pl.BlockSpec(memory_space=pl.ANY),
                 pl.BlockSpec((n_sample, D_INNER), lambda b, t: (0, 0)))
    stream_blk = (STREAM_SEQS, MLSTM_HEADS, MLSTM_HEAD_DIM, MLSTM_HEAD_DIM)
    scratch = [pltpu.VMEM((D_MODEL // LANES, tb, LANES), F32),
               pltpu.VMEM((tb, D_MODEL), BF16),
               pltpu.VMEM((SSD_CONV_DIM // CONV_TILE, WRAP, CONV_TILE), F32),
               pltpu.VMEM((tb, D_INNER), F32),
               pltpu.VMEM((tb, SSD_GN), F32),
               pltpu.VMEM((tb, SSD_GN), F32),
               pltpu.VMEM((tb, LANES), F32),
               pltpu.VMEM((tb, LANES), F32),
               pltpu.VMEM((tb, D_INNER), F32),
               pltpu.VMEM((SSD_STATE, D_INNER), F32),
               pltpu.VMEM(stream_blk, F32),
               pltpu.VMEM(stream_blk, F32),
               pltpu.SemaphoreType.DMA(()),
               pltpu.SemaphoreType.DMA(())]
    scratch += [pltpu.VMEM((tb // SSD_CHUNK, WRAP + SSD_CHUNK, CONV_TILE), F32)
                for _ in range(SSD_CONV_DIM // CONV_TILE)]
    return pl.pallas_call(
        _ssd_prompt_kernel, grid=grid, in_specs=in_specs, out_specs=out_specs,
        out_shape=out_shape, scratch_shapes=scratch, name="ssd_prompt",
        compiler_params=pltpu.CompilerParams(
            dimension_semantics=("arbitrary", "arbitrary"), vmem_limit_bytes=VMEM_LIMIT),
    )(cd_flat, x, *consts, c_state, kt, wv, qs)


def _pad_lanes(a, n=LANES):
    return jnp.pad(a, ((0, 0), (0, n - a.shape[-1])))


L1_TB = 256


def _log_sigmoid(x):
    return jnp.minimum(x, 0.0) - jnp.log1p(jnp.exp(-jnp.abs(x)))


def _dot_nt(a, b):
    return lax.dot_general(a, b, (((1,), (1,)), ((), ())), preferred_element_type=F32)


def _dot_tn(a, b):
    return lax.dot_general(a, b, (((0,), (0,)), ((), ())), preferred_element_type=F32)


def _mlstm_prompt_kernel(x_ref, prew_ref, win_ref, convw_ref, convb_ref,
                         wq_ref, wk_ref, wv_ref, wg_ref, bg_ref, hnw_ref, skip_ref, wout_ref,
                         postw_ref,
                         y_ref, convst_ref, c_ref, n_ref, m_ref,
                         u_scr, xe_scr, xc_scr, qkv_scr, qs_scr, h_scr, hb_scr, n_scr, m_scr):
    tb = L1_TB
    hd_dim = MLSTM_HEAD_DIM
    t = pl.program_id(1)
    nt = pl.num_programs(1)
    wxm_ref, wz_ref, wo_ref = _mlstm_win_views(win_ref)

    @pl.when(t == 0)
    def _():
        xe_scr[0:HALO, :] = jnp.zeros((HALO, D_INNER), F32)
        c_ref[...] = jnp.zeros_like(c_ref)
        n_scr[...] = jnp.zeros_like(n_scr)
        m_scr[...] = jnp.zeros_like(m_scr)

    x = x_ref[0]
    u = _rms(x, prew_ref[...]).astype(BF16)
    u_scr[...] = u

    nt_cols = 512
    for n0 in range(0, D_INNER, nt_cols):
        xe_scr[HALO:HALO + tb, n0:n0 + nt_cols] = _dot(u, wxm_ref[:, n0:n0 + nt_cols])

    for n0 in range(0, D_INNER, nt_cols):
        acc = convb_ref[:, n0:n0 + nt_cols]
        for k in range(CONV_WIDTH):
            off = HALO - (CONV_WIDTH - 1) + k
            acc = acc + xe_scr[off:off + tb, n0:n0 + nt_cols] * convw_ref[k:k + 1, n0:n0 + nt_cols]
        xc_scr[:, n0:n0 + nt_cols] = _silu(acc)

    scale = hd_dim ** -0.5
    for hd in range(MLSTM_HEADS):
        cols = slice(hd * hd_dim, (hd + 1) * hd_dim)
        xc_h = xc_scr[:, cols].astype(BF16)
        xm_h = xe_scr[HALO:HALO + tb, cols].astype(BF16)
        q_h = _dot(xc_h, wq_ref[hd])
        qkv_scr[:, cols] = q_h.astype(BF16)
        qs_scr[:, cols] = (q_h * scale).astype(BF16)
        qkv_scr[:, D_INNER + hd * hd_dim:D_INNER + (hd + 1) * hd_dim] = _dot(xc_h, wk_ref[hd]).astype(BF16)
        qkv_scr[:, 2 * D_INNER + hd * hd_dim:2 * D_INNER + (hd + 1) * hd_dim] = _dot(xm_h, wv_ref[hd]).astype(BF16)

    @pl.when(t == nt - 1)
    def _():
        convst_ref[0] = xe_scr[HALO + tb - (CONV_WIDTH - 1):HALO + tb, :]

    xe_scr[0:HALO, :] = xe_scr[tb:tb + HALO, :]

    gates = _dot(qkv_scr[...], wg_ref[...]) + bg_ref[...]
    li_all = gates
    lf_all = pltpu.roll(_log_sigmoid(gates), LANES - MLSTM_HEADS, axis=1)
    causal = _tri(tb)
    bcum = _dot_exact(causal.astype(F32), lf_all)
    m_row = m_scr[0:1, :]
    inter_all = bcum + m_row
    bcum_t = bcum.T
    li_t = li_all.T
    lane_row = lax.broadcasted_iota(jnp.int32, (1, LANES), 1)
    m_new_row = m_row

    for hd in range(MLSTM_HEADS):
        cols = slice(hd * hd_dim, (hd + 1) * hd_dim)
        kcols = slice(D_INNER + hd * hd_dim, D_INNER + (hd + 1) * hd_dim)
        vcols = slice(2 * D_INNER + hd * hd_dim, 2 * D_INNER + (hd + 1) * hd_dim)
        b_col = bcum[:, hd:hd + 1]
        dmat = b_col + (li_t[hd:hd + 1, :] - bcum_t[hd:hd + 1, :])
        dmat = jnp.where(causal, dmat, -jnp.inf)
        inter = inter_all[:, hd:hd + 1]
        m_t = jnp.maximum(inter, jnp.max(dmat, axis=1, keepdims=True))
        w = jnp.exp(dmat - m_t)
        qs_h = qs_scr[:, cols]
        k_h = qkv_scr[:, kcols]
        v_h = qkv_scr[:, vcols]
        s = _dot_nt(qs_h, k_h) * w
        inter_w = jnp.exp(inter - m_t)
        c_old = c_ref[0, hd]
        n_row = n_scr[0:1, cols]
        num = _dot(s.astype(BF16), v_h) + inter_w * _dot(qs_h, c_old.astype(BF16))
        qn = jnp.sum(qs_h.astype(F32) * n_row, axis=1, keepdims=True)
        den = jnp.sum(s, axis=1, keepdims=True) + inter_w * qn
        hh = num / jnp.maximum(jnp.abs(den), jnp.exp(-m_t))
        hc = hh - jnp.mean(hh, axis=1, keepdims=True)
        h_scr[:, cols] = hc * lax.rsqrt(jnp.mean(hc * hc, axis=1, keepdims=True) + EPS)
        m_new = m_t[tb - 1:tb, :]
        b_last = b_col[tb - 1:tb, :]
        wts = jnp.exp(b_last - b_col + li_all[:, hd:hd + 1] - m_new)
        cd = jnp.exp(b_last + m_row[:, hd:hd + 1] - m_new)
        kw = k_h.astype(F32) * wts
        c_ref[0, hd] = cd * c_old + _dot_tn(kw.astype(BF16), v_h)
        n_scr[0:1, cols] = cd * n_row + jnp.sum(kw, axis=0, keepdims=True)
        m_new_row = jnp.where(lane_row == hd, m_new, m_new_row)

    m_scr[0:1, :] = m_new_row

    @pl.when(t == nt - 1)
    def _():
        n_ref[0] = n_scr[0:1, :]
        m_ref[0] = m_new_row

    for n0 in range(0, D_INNER, nt_cols):
        cols = slice(n0, n0 + nt_cols)
        o = _sigmoid(_dot(u_scr[...], wo_ref[:, cols]))
        z = _dot(u_scr[...], wz_ref[:, cols])
        hg = o * (h_scr[:, cols] * hnw_ref[:, cols]) + skip_ref[:, cols] * xc_scr[:, cols]
        hb_scr[:, cols] = (hg * _silu(z)).astype(BF16)
    out = _dot(hb_scr[...], wout_ref[...])
    y_ref[0] = x_ref[0] + _rms(out, postw_ref[...])


def _mlstm_prompt(x, prew, win, convw, convb, wq, wk, wv, wg, bg, hnw, skip, wout, postw):
    bsz, seq, _ = x.shape
    tb = L1_TB
    grid = (bsz, seq // tb)
    consts = (prew, win, convw, convb, wq, wk, wv, wg, bg, hnw, skip, wout, postw)
    in_specs = [pl.BlockSpec((1, tb, D_MODEL), lambda b, t: (b, t, 0))]
    in_specs += [_const_spec(c.shape) for c in consts]
    hd_dim = MLSTM_HEAD_DIM
    out_shape = (jax.ShapeDtypeStruct((bsz, seq, D_MODEL), F32),
                 jax.ShapeDtypeStruct((bsz, CONV_WIDTH - 1, D_INNER), F32),
                 jax.ShapeDtypeStruct((bsz, MLSTM_HEADS, hd_dim, hd_dim), F32),
                 jax.ShapeDtypeStruct((bsz, 1, D_INNER), F32),
                 jax.ShapeDtypeStruct((bsz, 1, LANES), F32))
    out_specs = (pl.BlockSpec((1, tb, D_MODEL), lambda b, t: (b, t, 0)),
                 pl.BlockSpec((1, CONV_WIDTH - 1, D_INNER), lambda b, t: (b, 0, 0)),
                 pl.BlockSpec((1, MLSTM_HEADS, hd_dim, hd_dim), lambda b, t: (b, 0, 0, 0)),
                 pl.BlockSpec((1, 1, D_INNER), lambda b, t: (b, 0, 0)),
                 pl.BlockSpec((1, 1, LANES), lambda b, t: (b, 0, 0)))
    scratch = [pltpu.VMEM((tb, D_MODEL), BF16),
               pltpu.VMEM((tb + HALO, D_INNER), F32),
               pltpu.VMEM((tb, D_INNER), F32),
               pltpu.VMEM((tb, 3 * D_INNER), BF16),
               pltpu.VMEM((tb, D_INNER), BF16),
               pltpu.VMEM((tb, D_INNER), F32),
               pltpu.VMEM((tb, D_INNER), BF16),
               pltpu.VMEM((SUBLANES, D_INNER), F32),
               pltpu.VMEM((SUBLANES, LANES), F32)]
    return pl.pallas_call(
        _mlstm_prompt_kernel, grid=grid, in_specs=in_specs, out_specs=out_specs,
        out_shape=out_shape, scratch_shapes=scratch, name="mlstm_prompt",
        compiler_params=pltpu.CompilerParams(
            dimension_semantics=("arbitrary", "arbitrary"), vmem_limit_bytes=VMEM_LIMIT),
    )(x, *consts)


N_SAMPLE = 128
S0_SEQS = 4


def _split3(v):
    hi = v.astype(BF16)
    r1 = v - hi.astype(F32)
    mid = r1.astype(BF16)
    lo = (r1 - mid.astype(F32)).astype(BF16)
    return hi, mid, lo


def _expand_heads(v, width):
    rr = lax.broadcasted_iota(jnp.int32, (3 * LANES, D_INNER), 0) & (LANES - 1)
    cc = lax.broadcasted_iota(jnp.int32, (3 * LANES, D_INNER), 1) // width
    e3 = jnp.where(rr == cc, 1.0, 0.0).astype(BF16)
    return _dot(jnp.concatenate(_split3(v), axis=1), e3)


def _ssd_sample_pre_kernel(x_ref, cs_ref, prew_ref, win_ref, wdt_ref, convw_ref, convb_ref,
                           dtb_ref, alog_ref,
                           z_ref, xact_ref, b_ref, ct_ref, xdtt_ref, dec_ref, csnew_ref):
    cd = SSD_CONV_DIM
    wz_ref, wxbc_ref = _ssd_win_views(win_ref)
    u = _rms(x_ref[...], prew_ref[...]).astype(BF16)
    xbc = _dot(u, wxbc_ref[...])
    acc = convb_ref[...] + xbc * convw_ref[CONV_WIDTH - 1:CONV_WIDTH, :]
    for k in range(CONV_WIDTH - 1):
        acc = acc + cs_ref[:, k * cd:(k + 1) * cd] * convw_ref[k:k + 1, :]
    act = _silu(acc)
    csnew_ref[:, 0:2 * cd] = cs_ref[:, cd:3 * cd]
    csnew_ref[:, 2 * cd:3 * cd] = xbc
    xact = act[:, :D_INNER]
    dt = _softplus(_dot(u, wdt_ref[...]) + dtb_ref[...])
    dec_ref[...] = jnp.exp(dt * (-jnp.exp(alog_ref[...])))
    xdt = xact * _expand_heads(dt, SSD_HEAD_DIM)
    xdtt_ref[...] = xdt.T.astype(BF16)
    z_ref[...] = _dot(u, wz_ref[...])
    xact_ref[...] = xact
    b_ref[...] = act[:, D_INNER:D_INNER + SSD_GN]
    ct_ref[...] = act[:, D_INNER + SSD_GN:].T


def _ssd_sample_pre(x, cs, prew, win, wdt, convw, convb, dtb, alog):
    n = x.shape[0]
    out_shape = (jax.ShapeDtypeStruct((n, D_INNER), F32),
                 jax.ShapeDtypeStruct((n, D_INNER), F32),
                 jax.ShapeDtypeStruct((n, SSD_GN), F32),
                 jax.ShapeDtypeStruct((SSD_GN, n), F32),
                 jax.ShapeDtypeStruct((D_INNER, n), BF16),
                 jax.ShapeDtypeStruct((n, LANES), F32),
                 jax.ShapeDtypeStruct((n, (CONV_WIDTH - 1) * SSD_CONV_DIM), F32))
    return pl.pallas_call(
        _ssd_sample_pre_kernel, out_shape=out_shape, name="ssd_sample_pre",
        compiler_params=pltpu.CompilerParams(vmem_limit_bytes=VMEM_LIMIT),
    )(x, cs, prew, win, wdt, convw, convb, dtb, alog)


def _ssd_sample_state_kernel(dec_ref, h_ref, xdtt_ref, b_ref, ct_ref, hnew_ref, yt_ref):
    j = pl.program_id(0)
    gw = D_INNER // SSD_GROUPS
    hpg = SSD_HEADS // SSD_GROUPS

    @pl.when(j == 0)
    def _():
        yt_ref[...] = jnp.zeros_like(yt_ref)

    rowid = lax.broadcasted_iota(jnp.int32, (N_SAMPLE, SSD_STATE), 0)
    colid = lax.broadcasted_iota(jnp.int32, (SSD_STATE, N_SAMPLE), 1)
    for pp in range(S0_SEQS // 2):
        seqs = (j * S0_SEQS + 2 * pp, j * S0_SEQS + 2 * pp + 1)
        for g in range(SSD_GROUPS):
            bg = b_ref[:, g * SSD_STATE:(g + 1) * SSD_STATE]
            ctg = ct_ref[g * SSD_STATE:(g + 1) * SSD_STATE, :]
            rb = jnp.concatenate([jnp.where(rowid == s, bg, 0.0) for s in seqs], axis=1).astype(BF16)
            upd = _dot(xdtt_ref[g * gw:(g + 1) * gw, :], rb)
            for i, s in enumerate(seqs):
                for r in range(hpg):
                    rows = slice(g * gw + r * SSD_HEAD_DIM, g * gw + (r + 1) * SSD_HEAD_DIM)
                    dec = dec_ref[s * SSD_HEADS + g * hpg + r]
                    hn = (h_ref[2 * pp + i, rows, :] * dec
                          + upd[r * SSD_HEAD_DIM:(r + 1) * SSD_HEAD_DIM, i * SSD_STATE:(i + 1) * SSD_STATE])
                    hnew_ref[2 * pp + i, rows, :] = hn
            hn_pair = jnp.concatenate(
                [hnew_ref[2 * pp + i, g * gw:(g + 1) * gw, :] for i in range(2)], axis=1).astype(BF16)
            wc = jnp.concatenate([jnp.where(colid == s, ctg, 0.0) for s in seqs], axis=0).astype(BF16)
            yt_ref[g * gw:(g + 1) * gw, :] += _dot(hn_pair, wc)


def _ssd_sample_state(dec_flat, h, xdtt, b, ct):
    n = h.shape[0]
    grid = (n // S0_SEQS,)
    in_specs = [pl.BlockSpec(memory_space=pltpu.SMEM),
                pl.BlockSpec((S0_SEQS, D_INNER, SSD_STATE), lambda j: (j, 0, 0)),
                _const_spec(xdtt.shape), _const_spec(b.shape), _const_spec(ct.shape)]
    out_shape = (jax.ShapeDtypeStruct(h.shape, F32),
                 jax.ShapeDtypeStruct((D_INNER, n), F32))
    out_specs = (pl.BlockSpec((S0_SEQS, D_INNER, SSD_STATE), lambda j: (j, 0, 0)),
                 pl.BlockSpec((D_INNER, n), lambda j: (0, 0)))
    return pl.pallas_call(
        _ssd_sample_state_kernel, grid=grid, in_specs=in_specs, out_specs=out_specs,
        out_shape=out_shape, name="ssd_sample_state",
        compiler_params=pltpu.CompilerParams(
            dimension_semantics=("arbitrary",), vmem_limit_bytes=VMEM_LIMIT),
    )(dec_flat, h, xdtt, b, ct)


def _head_cols(a, hd):
    return a[:, hd * MLSTM_HEAD_DIM:(hd + 1) * MLSTM_HEAD_DIM]


def _sample_mid_kernel(x_ref, yt_ref, xact_ref, z_ref, dexp_ref, normw_ref, wout0_ref, postw0_ref,
                       mcs_ref, n0_ref, m0_ref, prew1_ref, win1_ref, convw_ref,
                       convb_ref, wq_ref, wk_ref, wv_ref, wg_ref, bg_ref,
                       x1_ref, z1_ref, o1_ref, xc_ref, v_ref, kt_ref, wv_out_ref, qs_ref,
                       s_ref, iw_ref, hden_ref, mcsnew_ref, nnew_ref, mnew_ref):
    gw = D_INNER // SSD_GROUPS
    wxm_ref, wz1_ref, wo1_ref = _mlstm_win_views(win1_ref)
    y = yt_ref[...].T + dexp_ref[...] * xact_ref[...]
    y = y * _silu(z_ref[...])
    parts = []
    for g in range(SSD_GROUPS):
        cols = slice(g * gw, (g + 1) * gw)
        parts.append(_rms(y[:, cols], normw_ref[:, cols]).astype(BF16))
    out0 = _dot(jnp.concatenate(parts, axis=1), wout0_ref[...])
    x1 = x_ref[...] + _rms(out0, postw0_ref[...])
    x1_ref[...] = x1

    u = _rms(x1, prew1_ref[...]).astype(BF16)
    xm = _dot(u, wxm_ref[...])
    z1_ref[...] = _dot(u, wz1_ref[...])
    o1_ref[...] = _dot(u, wo1_ref[...])
    acc = convb_ref[...] + xm * convw_ref[CONV_WIDTH - 1:CONV_WIDTH, :]
    for k in range(CONV_WIDTH - 1):
        acc = acc + mcs_ref[:, k * D_INNER:(k + 1) * D_INNER] * convw_ref[k:k + 1, :]
    xc = _silu(acc)
    xc_ref[...] = xc
    mcsnew_ref[:, 0:2 * D_INNER] = mcs_ref[:, D_INNER:3 * D_INNER]
    mcsnew_ref[:, 2 * D_INNER:3 * D_INNER] = xm

    scale = MLSTM_HEAD_DIM ** -0.5
    qs, ks, vs = [], [], []
    for hd in range(MLSTM_HEADS):
        xc_h = _head_cols(xc, hd).astype(BF16)
        xm_h = _head_cols(xm, hd).astype(BF16)
        qs.append(_dot(xc_h, wq_ref[hd]))
        ks.append(_dot(xc_h, wk_ref[hd]))
        vs.append(_dot(xm_h, wv_ref[hd]))
    qkv = jnp.concatenate(qs + ks + vs, axis=1).astype(BF16)
    gates = _dot(qkv, wg_ref[...]) + bg_ref[...]
    li = gates
    lf = pltpu.roll(_log_sigmoid(gates), LANES - MLSTM_HEADS, axis=1)
    m0 = m0_ref[...]
    inter = lf + m0
    m_t = jnp.maximum(inter, li)
    w = jnp.exp(li - m_t)
    iw = jnp.exp(inter - m_t)
    mnew_ref[...] = m_t

    lane = lax.broadcasted_iota(jnp.int32, (N_SAMPLE, LANES), 1)
    qk_arr = jnp.zeros((N_SAMPLE, LANES), F32)
    qn_arr = jnp.zeros((N_SAMPLE, LANES), F32)
    kb_parts = []
    for hd in range(MLSTM_HEADS):
        cols = slice(hd * MLSTM_HEAD_DIM, (hd + 1) * MLSTM_HEAD_DIM)
        q_h = (qs[hd] * scale).astype(BF16).astype(F32)
        k_h = ks[hd].astype(BF16).astype(F32)
        n_h = n0_ref[:, cols]
        qk_arr = jnp.where(lane == hd, jnp.sum(q_h * k_h, axis=1, keepdims=True), qk_arr)
        qn_arr = jnp.where(lane == hd, jnp.sum(q_h * n_h, axis=1, keepdims=True), qn_arr)
        w_h = w[:, hd:hd + 1]
        iw_h = iw[:, hd:hd + 1]
        nnew_ref[:, cols] = iw_h * n_h + w_h * k_h
        wv_out_ref[:, cols] = w_h * vs[hd]
        v_ref[:, cols] = vs[hd]
        qs_ref[:, cols] = q_h
        kb_parts.append(k_h)
    kt_ref[...] = jnp.concatenate(kb_parts, axis=1).T.astype(BF16)
    s = qk_arr * w
    den = s + iw * qn_arr
    s_ref[...] = s
    iw_ref[...] = iw
    hden_ref[...] = jnp.maximum(jnp.abs(den), jnp.exp(-m_t))


def _sample_mid(*args):
    n = N_SAMPLE
    wide = jax.ShapeDtypeStruct((n, D_INNER), F32)
    small = jax.ShapeDtypeStruct((n, LANES), F32)
    out_shape = (jax.ShapeDtypeStruct((n, D_MODEL), F32),
                 wide, wide, wide, wide,
                 jax.ShapeDtypeStruct((D_INNER, n), BF16),
                 wide, wide,
                 small, small, small,
                 jax.ShapeDtypeStruct((n, (CONV_WIDTH - 1) * D_INNER), F32),
                 wide, small)
    return pl.pallas_call(
        _sample_mid_kernel, out_shape=out_shape, name="sample_mid",
        compiler_params=pltpu.CompilerParams(vmem_limit_bytes=VMEM_LIMIT),
    )(*args)


def _mlstm_sample_post_kernel(x1_ref, num_ref, v_ref, s_ref, iw_ref, hden_ref, o1_ref, z1_ref,
                              xc_ref, hnw_ref, skip_ref, wout_ref, postw_ref, y_ref):
    parts = []
    for hd in range(MLSTM_HEADS):
        cols = slice(hd * MLSTM_HEAD_DIM, (hd + 1) * MLSTM_HEAD_DIM)
        num = s_ref[:, hd:hd + 1] * v_ref[:, cols] + iw_ref[:, hd:hd + 1] * num_ref[:, cols]
        hh = num / hden_ref[:, hd:hd + 1]
        hc = hh - jnp.mean(hh, axis=1, keepdims=True)
        hn = hc * lax.rsqrt(jnp.mean(hc * hc, axis=1, keepdims=True) + EPS)
        hg = _sigmoid(o1_ref[:, cols]) * (hn * hnw_ref[:, cols]) + skip_ref[:, cols] * xc_ref[:, cols]
        parts.append((hg * _silu(z1_ref[:, cols])).astype(BF16))
    out = _dot(jnp.concatenate(parts, axis=1), wout_ref[...])
    y_ref[...] = x1_ref[...] + _rms(out, postw_ref[...])


def _mlstm_sample_post(*args):
    return pl.pallas_call(
        _mlstm_sample_post_kernel, out_shape=jax.ShapeDtypeStruct((N_SAMPLE, D_MODEL), F32),
        name="mlstm_sample_post",
        compiler_params=pltpu.CompilerParams(vmem_limit_bytes=VMEM_LIMIT),
    )(*args)


def kernel(x_prompt, x_sample, state_ssd_conv, state_ssd, state_mlstm_conv, state_mlstm_c, state_mlstm_n, state_mlstm_m, pre_norm_w, post_norm_w, ssd_w_in, ssd_conv_w, ssd_conv_b, ssd_dt_bias, ssd_a_log, ssd_d, ssd_norm_w, ssd_w_out, ml_w_in, ml_conv_w, ml_conv_b, ml_w_q, ml_w_k, ml_w_v, ml_w_gate, ml_b_gate, ml_head_norm_w, ml_skip, ml_w_out):
    win0 = ssd_w_in[0].astype(BF16)
    wdt = _pad_lanes(win0[:, D_INNER + SSD_CONV_DIM:])
    dtb = _pad_lanes(ssd_dt_bias[0][None])
    alog = _pad_lanes(ssd_a_log[0][None])
    dexp = jnp.repeat(ssd_d[0], SSD_HEAD_DIM)[None]
    prew0, postw0 = pre_norm_w[0][None], post_norm_w[0][None]
    prew1, postw1 = pre_norm_w[1][None], post_norm_w[1][None]
    convw0, convb0 = ssd_conv_w[0], ssd_conv_b[0][None]
    normw0 = ssd_norm_w[0][None]
    wout0 = ssd_w_out[0].astype(BF16)
    win1 = ml_w_in[0].astype(BF16)
    wg = _pad_lanes(ml_w_gate[0]).astype(BF16)
    bg = _pad_lanes(ml_b_gate[0][None])
    convw1, convb1 = ml_conv_w[0], ml_conv_b[0][None]
    wq, wk, wv = ml_w_q[0].astype(BF16), ml_w_k[0].astype(BF16), ml_w_v[0].astype(BF16)
    hnw, skip = ml_head_norm_w[0][None], ml_skip[0][None]
    wout1 = ml_w_out[0].astype(BF16)

    n = x_sample.shape[0]
    xs = x_sample.reshape(n, D_MODEL)
    cs0 = state_ssd_conv[0].reshape(n, (CONV_WIDTH - 1) * SSD_CONV_DIM)
    z0, xact0, b0, ct0, xdtt0, dec0, csnew0 = _ssd_sample_pre(
        xs, cs0, prew0, win0, wdt, convw0, convb0, dtb, alog)
    h0 = state_ssd[0].reshape(n, D_INNER, SSD_STATE)
    hnew0, yt0 = _ssd_sample_state(dec0[:, :SSD_HEADS].reshape(-1), h0, xdtt0, b0, ct0)
    mcs = state_mlstm_conv[0].reshape(n, (CONV_WIDTH - 1) * D_INNER)
    n0 = state_mlstm_n[0].reshape(n, D_INNER)
    m0 = _pad_lanes(state_mlstm_m[0])
    (x1, z1, o1, xc1, v1, kt1, wv1, qs1, s1, iw1, hden1, mcsnew, nnew, mnew) = _sample_mid(
        xs, yt0, xact0, z0, dexp, normw0, wout0, postw0,
        mcs, n0, m0, prew1, win1, convw1, convb1, wq, wk, wv, wg, bg)

    bp = x_prompt.shape[0]
    y0, p_sc, p_sh, cnew, num1 = _ssd_prompt(
        x_prompt, prew0, win0, wdt, convw0, convb0, dtb, alog, dexp, normw0, wout0, postw0,
        iw1[:, :MLSTM_HEADS].reshape(-1), state_mlstm_c[0], kt1, wv1, qs1)
    y_prompt, p_mc, p_c, p_n, p_m = _mlstm_prompt(
        y0, prew1, win1, convw1, convb1, wq, wk, wv, wg, bg, hnw, skip, wout1, postw1)
    p_sc = p_sc[None]
    p_sh = p_sh.reshape(1, bp, SSD_HEADS, SSD_HEAD_DIM, SSD_STATE)
    p_mc = p_mc[None]
    p_c = p_c[None]
    p_n = p_n.reshape(1, bp, MLSTM_HEADS, MLSTM_HEAD_DIM)
    p_m = p_m[:, 0, :MLSTM_HEADS][None]

    y_s = _mlstm_sample_post(x1, num1, v1, s1, iw1, hden1, o1, z1, xc1, hnw, skip, wout1, postw1)

    y_sample = y_s.reshape(n, 1, D_MODEL)
    s_sc = csnew0.reshape(1, n, CONV_WIDTH - 1, SSD_CONV_DIM)
    s_sh = hnew0.reshape(1, n, SSD_HEADS, SSD_HEAD_DIM, SSD_STATE)
    s_mc = mcsnew.reshape(1, n, CONV_WIDTH - 1, D_INNER)
    s_c = cnew[None]
    s_n = nnew.reshape(1, n, MLSTM_HEADS, MLSTM_HEAD_DIM)
    s_m = mnew[:, :MLSTM_HEADS][None]
    return (y_prompt, y_sample, p_sc, s_sc, p_sh, s_sh, p_mc, s_mc, p_c, s_c, p_n, s_n, p_m, s_m)
```

```python
import jax
import jax.numpy as jnp
from jax import lax
from jax.experimental import pallas as pl
from jax.experimental.pallas import tpu as pltpu

F32 = jnp.float32
BF16 = jnp.bfloat16

D_MODEL = 1024
D_INNER = 2048
CONV_WIDTH = 4
SSD_HEAD_DIM = 64
SSD_HEADS = 32
SSD_GROUPS = 4
SSD_STATE = 128
SSD_GN = SSD_GROUPS * SSD_STATE
SSD_CONV_DIM = D_INNER + 2 * SSD_GN
SSD_CHUNK = 128
MLSTM_HEADS = 4
MLSTM_HEAD_DIM = 512
EPS = 1e-6

LANES = 128
SUBLANES = 8
HALO = SUBLANES
VMEM_LIMIT = 58 * 1024 * 1024

L0_TB = 256


def _sigmoid(x):
    return 0.5 * jnp.tanh(0.5 * x) + 0.5


def _silu(x):
    h = 0.5 * x
    return h * jnp.tanh(h) + h


def _softplus(x):
    return jnp.maximum(x, 0.0) + jnp.log1p(jnp.exp(-jnp.abs(x)))


def _rms(x, w):
    ms = jnp.mean(x * x, axis=-1, keepdims=True)
    return x * lax.rsqrt(ms + EPS) * w


def _dot(a, b):
    return jnp.dot(a, b, preferred_element_type=F32)


def _dot_exact(a, b):
    return jnp.dot(a, b, preferred_element_type=F32, precision=lax.Precision.HIGHEST)


def _tri(n):
    r = lax.broadcasted_iota(jnp.int32, (n, n), 0)
    c = lax.broadcasted_iota(jnp.int32, (n, n), 1)
    return r >= c


WRAP = (CONV_WIDTH - 1) * SUBLANES
CONV_TILE = 512
LOG2E = 1.4426950408889634


def _stage_columns(x_ref, xs_scr):
    for k in range(xs_scr.shape[0]):
        xs_scr[k] = x_ref[0, :, k * LANES:(k + 1) * LANES]


def _load_permuted(xs_scr, tb, q):
    nv = q // SUBLANES
    cols = []
    for k in range(xs_scr.shape[0]):
        parts = [xs_scr[k, pl.ds(c * q + j, SUBLANES, stride=nv), :]
                 for c in range(tb // q) for j in range(nv)]
        cols.append(jnp.concatenate(parts, axis=0))
    return jnp.concatenate(cols, axis=1)


def _store_unpermuted(y_ref, xs_scr, y, tb, q):
    nv = q // SUBLANES
    for k in range(xs_scr.shape[0]):
        for c in range(tb // q):
            for j in range(nv):
                r0 = c * q + j * SUBLANES
                xs_scr[k, pl.ds(c * q + j, SUBLANES, stride=nv), :] = y[r0:r0 + SUBLANES,
                                                                        k * LANES:(k + 1) * LANES]
    for k in range(xs_scr.shape[0]):
        y_ref[0, :, k * LANES:(k + 1) * LANES] = xs_scr[k]


def _perm_causal(n, q):
    nv = q // SUBLANES
    r = lax.broadcasted_iota(jnp.int32, (n, n), 0)
    c = lax.broadcasted_iota(jnp.int32, (n, n), 1)
    tok = lambda p: (p & -q) + (p & (SUBLANES - 1)) * nv + ((p & (q - 1)) >> 3)
    return tok(r) >= tok(c)


def _fill_wrap_rows(xe_t, hist_t, nchunk, q):
    sub = lax.broadcasted_iota(jnp.int32, (SUBLANES, xe_t.shape[-1]), 0)
    for c in range(nchunk):
        for i in range(CONV_WIDTH - 1):
            src = slice(q + i * SUBLANES, q + (i + 1) * SUBLANES)
            cur = xe_t[c, src, :]
            prev = hist_t[i * SUBLANES:(i + 1) * SUBLANES, :] if c == 0 else xe_t[c - 1, src, :]
            xe_t[c, i * SUBLANES:(i + 1) * SUBLANES, :] = jnp.where(
                sub == 0, pltpu.roll(prev, 1, axis=0), pltpu.roll(cur, 1, axis=0))
    hist_t[...] = xe_t[nchunk - 1, q:q + WRAP, :]


def _conv_tile(xe_t, convw_ref, convb_ref, cols, c, q):
    acc = convb_ref[:, cols]
    for k in range(CONV_WIDTH):
        acc = acc + xe_t[c, k * SUBLANES:k * SUBLANES + q, :] * convw_ref[k:k + 1, cols]
    return acc


def _store_conv_state(convst_ref, xe_scr, nchunk, q, tile):
    for n in range(len(xe_scr)):
        for i in range(CONV_WIDTH - 1):
            row = q + (i + 1) * SUBLANES - 1
            convst_ref[0, i:i + 1, n * tile:(n + 1) * tile] = xe_scr[n][nchunk - 1, row:row + 1, :]


def _ssd_win_views(win_ref):
    return win_ref.at[:, 0:D_INNER], win_ref.at[:, D_INNER:D_INNER + SSD_CONV_DIM]


def _mlstm_win_views(win_ref):
    return (win_ref.at[:, 0:D_INNER], win_ref.at[:, D_INNER:2 * D_INNER],
            win_ref.at[:, 2 * D_INNER:3 * D_INNER])


def _const_spec(shape):
    nd = len(shape)
    return pl.BlockSpec(shape, lambda *_: (0,) * nd, pipeline_mode=pl.Buffered(1))


STREAM_SEQS = 2


class _MlstmStateStream:
    def __init__(self, step, nsteps, cd_ref, c_hbm, kt_ref, wv_ref, qs_ref, cnew_hbm, num_ref,
                 cin_scr, cout_scr, sem_in, sem_out):
        self.step, self.nsteps = step, nsteps
        self.cd_ref, self.kt_ref, self.wv_ref, self.qs_ref = cd_ref, kt_ref, wv_ref, qs_ref
        self.num_ref, self.cin_scr, self.cout_scr = num_ref, cin_scr, cout_scr
        self.in_copy = lambda s: pltpu.make_async_copy(
            c_hbm.at[pl.ds(s * STREAM_SEQS, STREAM_SEQS)], cin_scr, sem_in)
        self.out_copy = lambda s: pltpu.make_async_copy(
            cout_scr, cnew_hbm.at[pl.ds(s * STREAM_SEQS, STREAM_SEQS)], sem_out)

    def begin(self):
        step = self.step

        @pl.when(step == 0)
        def _():
            self.in_copy(0).start()

        self.in_copy(step).wait()

        @pl.when(step > 0)
        def _():
            self.out_copy(step - 1).wait()

    def update(self, i, hd):
        b = self.step * STREAM_SEQS + i
        b8 = pl.multiple_of((b // SUBLANES) * SUBLANES, SUBLANES)
        rowid = lax.broadcasted_iota(jnp.int32, (N_SAMPLE, MLSTM_HEAD_DIM), 0)
        sub = lax.broadcasted_iota(jnp.int32, (SUBLANES, MLSTM_HEAD_DIM), 0)
        cols = slice(hd * MLSTM_HEAD_DIM, (hd + 1) * MLSTM_HEAD_DIM)
        c_old = self.cin_scr[i, hd]
        rv = jnp.where(rowid == b, self.wv_ref[:, cols], 0.0).astype(BF16)
        upd = _dot(self.kt_ref[cols, :], rv)
        self.cout_scr[i, hd] = self.cd_ref[b * MLSTM_HEADS + hd] * c_old + upd
        q8 = self.qs_ref[pl.ds(b8, SUBLANES), cols].astype(BF16)
        r8 = _dot(q8, c_old.astype(BF16))
        self.num_ref[pl.ds(b, 1), cols] = jnp.sum(jnp.where(sub == b - b8, r8, 0.0), axis=0, keepdims=True)

    def end(self):
        step, nsteps = self.step, self.nsteps
        self.out_copy(step).start()

        @pl.when(step + 1 < nsteps)
        def _():
            self.in_copy(step + 1).start()

        @pl.when(step == nsteps - 1)
        def _():
            self.out_copy(step).wait()


def _ssd_prompt_kernel(cd_ref, x_ref, prew_ref, win_ref, wdt_ref, convw_ref, convb_ref,
                       dtb_ref, alog_ref, dexp_ref, normw_ref, wout_ref, postw_ref,
                       c_hbm, kt_ref, wv_ref, qs_ref,
                       y_ref, convst_ref, hst_ref, cnew_hbm, num_ref,
                       xs_scr, u_scr, hist_scr, xact_scr, b_scr, c_scr, dt_scr, da_scr, yb_scr,
                       st_scr, cin_scr, cout_scr, sem_in, sem_out, *xe_scr):
    tb = L0_TB
    t = pl.program_id(1)
    nt = pl.num_programs(1)
    wz_ref, wxbc_ref = _ssd_win_views(win_ref)

    stream = _MlstmStateStream(pl.program_id(0) * nt + t, pl.num_programs(0) * nt,
                               cd_ref, c_hbm, kt_ref, wv_ref, qs_ref, cnew_hbm, num_ref,
                               cin_scr, cout_scr, sem_in, sem_out)
    stream.begin()
    for i in range(STREAM_SEQS):
        for hd in range(MLSTM_HEADS):
            stream.update(i, hd)
    stream.end()

    q = SSD_CHUNK
    nchunk = tb // q

    @pl.when(t == 0)
    def _():
        hist_scr[...] = jnp.zeros_like(hist_scr)
        st_scr[...] = jnp.zeros_like(st_scr)

    _stage_columns(x_ref, xs_scr)
    x = _load_permuted(xs_scr, tb, q)
    u = _rms(x, prew_ref[...]).astype(BF16)
    u_scr[...] = u

    for n in range(SSD_CONV_DIM // CONV_TILE):
        res = _dot(u, wxbc_ref[:, n * CONV_TILE:(n + 1) * CONV_TILE])
        for c in range(nchunk):
            xe_scr[n][c, WRAP:WRAP + q, :] = res[c * q:(c + 1) * q]
    for n in range(SSD_CONV_DIM // CONV_TILE):
        n0 = n * CONV_TILE
        cols = slice(n0, n0 + CONV_TILE)
        xe_t = xe_scr[n]
        _fill_wrap_rows(xe_t, hist_scr.at[n], nchunk, q)
        for c in range(nchunk):
            rows = slice(c * q, (c + 1) * q)
            act = _silu(_conv_tile(xe_t, convw_ref, convb_ref, cols, c, q))
            if n0 < D_INNER:
                xact_scr[rows, cols] = act
            elif n0 < D_INNER + SSD_GN:
                b_scr[rows, n0 - D_INNER:n0 - D_INNER + CONV_TILE] = act
            else:
                c_scr[rows, n0 - D_INNER - SSD_GN:n0 - D_INNER - SSD_GN + CONV_TILE] = act

    @pl.when(t == nt - 1)
    def _():
        _store_conv_state(convst_ref, xe_scr, nchunk, q, CONV_TILE)

    dt = _softplus(_dot(u, wdt_ref[...].astype(BF16)) + dtb_ref[...])
    dt_scr[...] = dt
    da_scr[...] = dt * (-jnp.exp(alog_ref[...])) * LOG2E

    causal = _perm_causal(q, q)
    tri_f = causal.astype(F32)
    lane = lax.broadcasted_iota(jnp.int32, (q, LANES), 1)
    lane_lo = lane < SSD_HEAD_DIM
    hpg = SSD_HEADS // SSD_GROUPS
    gw = D_INNER // SSD_GROUPS

    for ci in range(nchunk):
        rows = slice(ci * q, (ci + 1) * q)
        cum = _dot_exact(tri_f, da_scr[rows, :])
        cum_t = cum.T
        dtc = dt_scr[rows, :]
        for g in range(SSD_GROUPS):
            bg = b_scr[rows, g * SSD_STATE:(g + 1) * SSD_STATE]
            cg = c_scr[rows, g * SSD_STATE:(g + 1) * SSD_STATE].astype(BF16)
            bg_t = bg.T.astype(BF16)
            cb = _dot(cg, bg_t)
            st_g = st_scr[:, g * gw:(g + 1) * gw]
            y_inter = _dot(cg, st_g.astype(BF16))
            xt_parts = []
            el_parts = []
            for jj in range(hpg // 2):
                j = g * (hpg // 2) + jj
                h0, h1 = 2 * j, 2 * j + 1
                col0 = jnp.broadcast_to(cum[:, h0:h0 + 1], (q, q))
                col1 = jnp.broadcast_to(cum[:, h1:h1 + 1], (q, q))
                row0 = jnp.broadcast_to(cum_t[h0:h0 + 1, :], (q, q))
                row1 = jnp.broadcast_to(cum_t[h1:h1 + 1, :], (q, q))
                dec0 = jnp.exp2(jnp.where(causal, col0 - row0, -jnp.inf))
                dec1 = jnp.exp2(jnp.where(causal, col1 - row1, -jnp.inf))
                m_pair = jnp.concatenate([(cb * dec0).astype(BF16), (cb * dec1).astype(BF16)], axis=1)
                col_p = jnp.where(lane_lo, col0, col1)
                dt_p = jnp.where(lane_lo,
                                 jnp.broadcast_to(dtc[:, h0:h0 + 1], (q, LANES)),
                                 jnp.broadcast_to(dtc[:, h1:h1 + 1], (q, LANES)))
                cols = slice(j * LANES, (j + 1) * LANES)
                xp = xact_scr[rows, cols]
                xdt = xp * dt_p
                x_bd = jnp.concatenate([jnp.where(lane_lo, xdt, 0.0).astype(BF16),
                                        jnp.where(lane_lo, 0.0, xdt).astype(BF16)], axis=0)
                y_p = _dot(m_pair, x_bd)
                y_p = y_p + y_inter[:, jj * LANES:(jj + 1) * LANES] * jnp.exp2(col_p)
                y_p = y_p + dexp_ref[:, cols] * xp
                yb_scr[rows, cols] = y_p
                last_p = col_p[q - 1:q, :]
                xt_parts.append((xdt * jnp.exp2(last_p - col_p)).astype(BF16))
                el_parts.append(jnp.exp2(last_p))
            xt = jnp.concatenate(xt_parts, axis=1)
            el = jnp.concatenate(el_parts, axis=1)
            st_scr[:, g * gw:(g + 1) * gw] = st_g * el + _dot(bg_t, xt)

    @pl.when(t == nt - 1)
    def _():
        for k in range(D_INNER // LANES):
            hst_ref[0, k * LANES:(k + 1) * LANES, :] = st_scr[:, k * LANES:(k + 1) * LANES].T

    for g in range(SSD_GROUPS):
        cols = slice(g * gw, (g + 1) * gw)
        z = _dot(u_scr[...], wz_ref[:, cols])
        yg = yb_scr[:, cols] * _silu(z)
        yg = _rms(yg, normw_ref[:, cols])
        yb_scr[:, cols] = yg
    out = _dot(yb_scr[...].astype(BF16), wout_ref[...])
    y_ref[0] = x + _rms(out, postw_ref[...])


def _ssd_prompt(x, prew, win, wdt, convw, convb, dtb, alog, dexp, normw, wout, postw,
                cd_flat, c_state, kt, wv, qs):
    bsz, seq, _ = x.shape
    tb = L0_TB
    grid = (bsz, seq // tb)
    n_sample = c_state.shape[0]
    assert n_sample == grid[0] * grid[1] * STREAM_SEQS
    consts = (prew, win, wdt, convw, convb, dtb, alog, dexp, normw, wout, postw)
    in_specs = [pl.BlockSpec(memory_space=pltpu.SMEM),
                pl.BlockSpec((1, tb, D_MODEL), lambda b, t: (b, t, 0))]
    in_specs += [_const_spec(c.shape) for c in consts]
    in_specs += [pl.BlockSpec(memory_space=pl.ANY),
                 _const_spec(kt.shape), _const_spec(wv.shape), _const_spec(qs.shape)]
    out_shape = (jax.ShapeDtypeStruct((bsz, seq, D_MODEL), F32),
                 jax.ShapeDtypeStruct((bsz, CONV_WIDTH - 1, SSD_CONV_DIM), F32),
                 jax.ShapeDtypeStruct((bsz, D_INNER, SSD_STATE), F32),
                 jax.ShapeDtypeStruct(c_state.shape, F32),
                 jax.ShapeDtypeStruct((n_sample, D_INNER), F32))
    out_specs = (pl.BlockSpec((1, tb, D_MODEL), lambda b, t: (b, t, 0)),
                 pl.BlockSpec((1, CONV_WIDTH - 1, SSD_CONV_DIM), lambda b, t: (b, 0, 0)),
                 pl.BlockSpec((1, D_INNER, SSD_STATE), lambda b, t: (b, 0, 0)),
                 pl.BlockSpec(memory_space=pl.ANY),
                 pl.BlockSpec((n_sample, D_INNER), lambda b, t: (0, 0)))
    stream_blk = (STREAM_SEQS, MLSTM_HEADS, MLSTM_HEAD_DIM, MLSTM_HEAD_DIM)
    scratch = [pltpu.VMEM((D_MODEL // LANES, tb, LANES), F32),
               pltpu.VMEM((tb, D_MODEL), BF16),
               pltpu.VMEM((SSD_CONV_DIM // CONV_TILE, WRAP, CONV_TILE), F32),
               pltpu.VMEM((tb, D_INNER), F32),
               pltpu.VMEM((tb, SSD_GN), F32),
               pltpu.VMEM((tb, SSD_GN), F32),
               pltpu.VMEM((tb, LANES), F32),
               pltpu.VMEM((tb, LANES), F32),
               pltpu.VMEM((tb, D_INNER), F32),
               pltpu.VMEM((SSD_STATE, D_INNER), F32),
               pltpu.VMEM(stream_blk, F32),
               pltpu.VMEM(stream_blk, F32),
               pltpu.SemaphoreType.DMA(()),
               pltpu.SemaphoreType.DMA(())]
    scratch += [pltpu.VMEM((tb // SSD_CHUNK, WRAP + SSD_CHUNK, CONV_TILE), F32)
                for _ in range(SSD_CONV_DIM // CONV_TILE)]
    return pl.pallas_call(
        _ssd_prompt_kernel, grid=grid, in_specs=in_specs, out_specs=out_specs,
        out_shape=out_shape, scratch_shapes=scratch, name="ssd_prompt",
        compiler_params=pltpu.CompilerParams(
            dimension_semantics=("arbitrary", "arbitrary"), vmem_limit_bytes=VMEM_LIMIT),
    )(cd_flat, x, *consts, c_state, kt, wv, qs)


def _pad_lanes(a, n=LANES):
    return jnp.pad(a, ((0, 0), (0, n - a.shape[-1])))


L1_TB = 256


def _log_sigmoid(x):
    return jnp.minimum(x, 0.0) - jnp.log1p(jnp.exp(-jnp.abs(x)))


def _dot_nt(a, b):
    return lax.dot_general(a, b, (((1,), (1,)), ((), ())), preferred_element_type=F32)


def _dot_tn(a, b):
    return lax.dot_general(a, b, (((0,), (0,)), ((), ())), preferred_element_type=F32)


def _mlstm_prompt_kernel(x_ref, prew_ref, win_ref, convw_ref, convb_ref,
                         wq_ref, wk_ref, wv_ref, wg_ref, bg_ref, hnw_ref, skip_ref, wout_ref,
                         postw_ref,
                         y_ref, convst_ref, c_ref, n_ref, m_ref,
                         xs_scr, u_scr, hist_scr, xc_scr, qkv_scr, qs_scr, h_scr, hb_scr, n_scr, m_scr,
                         *xe_scr):
    tb = L1_TB
    hd_dim = MLSTM_HEAD_DIM
    t = pl.program_id(1)
    nt = pl.num_programs(1)
    wxm_ref, wz_ref, wo_ref = _mlstm_win_views(win_ref)
    q = SSD_CHUNK
    nchunk = tb // q

    @pl.when(t == 0)
    def _():
        hist_scr[...] = jnp.zeros_like(hist_scr)
        c_ref[...] = jnp.zeros_like(c_ref)
        n_scr[...] = jnp.zeros_like(n_scr)
        m_scr[...] = jnp.zeros_like(m_scr)

    x = x_ref[0]
    u = _rms(x, prew_ref[...]).astype(BF16)
    u_scr[...] = u

    nt_cols = 512
    scale = hd_dim ** -0.5
    for hd in range(MLSTM_HEADS):
        res = _dot(u, wxm_ref[:, hd * hd_dim:(hd + 1) * hd_dim])
        for c in range(nchunk):
            xe_scr[hd][c, WRAP:WRAP + q, :] = res[c * q:(c + 1) * q]
    for hd in range(MLSTM_HEADS):
        cols = slice(hd * hd_dim, (hd + 1) * hd_dim)
        _fill_wrap_rows(xe_scr[hd], hist_scr.at[hd], nchunk, q)
        for c in range(nchunk):
            xc_scr[c * q:(c + 1) * q, cols] = _silu(_conv_tile(xe_scr[hd], convw_ref, convb_ref, cols, c, q))
    for hd in range(MLSTM_HEADS):
        cols = slice(hd * hd_dim, (hd + 1) * hd_dim)
        xe_t = xe_scr[hd]
        xc_h = xc_scr[:, cols].astype(BF16)
        xm_h = jnp.concatenate([xe_t[c, WRAP:WRAP + q, :] for c in range(nchunk)], axis=0).astype(BF16)
        q_h = _dot(xc_h, wq_ref[hd])
        qkv_scr[:, cols] = q_h.astype(BF16)
        qs_scr[:, cols] = (q_h * scale).astype(BF16)
        qkv_scr[:, D_INNER + hd * hd_dim:D_INNER + (hd + 1) * hd_dim] = _dot(xc_h, wk_ref[hd]).astype(BF16)
        qkv_scr[:, 2 * D_INNER + hd * hd_dim:2 * D_INNER + (hd + 1) * hd_dim] = _dot(xm_h, wv_ref[hd]).astype(BF16)

    @pl.when(t == nt - 1)
    def _():
        _store_conv_state(convst_ref, xe_scr, nchunk, q, hd_dim)

    gates = _dot(qkv_scr[...], wg_ref[...]) + bg_ref[...]
    li_all = gates
    lf_all = pltpu.roll(_log_sigmoid(gates), LANES - MLSTM_HEADS, axis=1)
    causal = _perm_causal(tb, q)
    bcum = _dot_exact(causal.astype(F32), lf_all)
    m_row = m_scr[0:1, :]
    inter_all = bcum + m_row
    bcum_t = bcum.T
    li_t = li_all.T
    lane_row = lax.broadcasted_iota(jnp.int32, (1, LANES), 1)
    m_new_row = m_row

    for hd in range(MLSTM_HEADS):
        cols = slice(hd * hd_dim, (hd + 1) * hd_dim)
        kcols = slice(D_INNER + hd * hd_dim, D_INNER + (hd + 1) * hd_dim)
        vcols = slice(2 * D_INNER + hd * hd_dim, 2 * D_INNER + (hd + 1) * hd_dim)
        b_col = bcum[:, hd:hd + 1]
        dmat = b_col + (li_t[hd:hd + 1, :] - bcum_t[hd:hd + 1, :])
        dmat = jnp.where(causal, dmat, -jnp.inf)
        inter = inter_all[:, hd:hd + 1]
        m_t = jnp.maximum(inter, jnp.max(dmat, axis=1, keepdims=True))
        w = jnp.exp(dmat - m_t)
        qs_h = qs_scr[:, cols]
        k_h = qkv_scr[:, kcols]
        v_h = qkv_scr[:, vcols]
        s = _dot_nt(qs_h, k_h) * w
        inter_w = jnp.exp(inter - m_t)
        c_old = c_ref[0, hd]
        n_row = n_scr[0:1, cols]
        num = _dot(s.astype(BF16), v_h) + inter_w * _dot(qs_h, c_old.astype(BF16))
        qn = jnp.sum(qs_h.astype(F32) * n_row, axis=1, keepdims=True)
        den = jnp.sum(s, axis=1, keepdims=True) + inter_w * qn
        hh = num / jnp.maximum(jnp.abs(den), jnp.exp(-m_t))
        hc = hh - jnp.mean(hh, axis=1, keepdims=True)
        h_scr[:, cols] = hc * lax.rsqrt(jnp.mean(hc * hc, axis=1, keepdims=True) + EPS)
        m_new = m_t[tb - 1:tb, :]
        b_last = b_col[tb - 1:tb, :]
        wts = jnp.exp(b_last - b_col + li_all[:, hd:hd + 1] - m_new)
        cd = jnp.exp(b_last + m_row[:, hd:hd + 1] - m_new)
        kw = k_h.astype(F32) * wts
        c_ref[0, hd] = cd * c_old + _dot_tn(kw.astype(BF16), v_h)
        n_scr[0:1, cols] = cd * n_row + jnp.sum(kw, axis=0, keepdims=True)
        m_new_row = jnp.where(lane_row == hd, m_new, m_new_row)

    m_scr[0:1, :] = m_new_row

    @pl.when(t == nt - 1)
    def _():
        n_ref[0] = n_scr[0:1, :]
        m_ref[0] = m_new_row

    for n0 in range(0, D_INNER, nt_cols):
        cols = slice(n0, n0 + nt_cols)
        o = _sigmoid(_dot(u_scr[...], wo_ref[:, cols]))
        z = _dot(u_scr[...], wz_ref[:, cols])
        hg = o * (h_scr[:, cols] * hnw_ref[:, cols]) + skip_ref[:, cols] * xc_scr[:, cols]
        hb_scr[:, cols] = (hg * _silu(z)).astype(BF16)
    out = _dot(hb_scr[...], wout_ref[...])
    _store_unpermuted(y_ref, xs_scr, x_ref[0] + _rms(out, postw_ref[...]), tb, q)


def _mlstm_prompt(x, prew, win, convw, convb, wq, wk, wv, wg, bg, hnw, skip, wout, postw):
    bsz, seq, _ = x.shape
    tb = L1_TB
    grid = (bsz, seq // tb)
    consts = (prew, win, convw, convb, wq, wk, wv, wg, bg, hnw, skip, wout, postw)
    in_specs = [pl.BlockSpec((1, tb, D_MODEL), lambda b, t: (b, t, 0))]
    in_specs += [_const_spec(c.shape) for c in consts]
    hd_dim = MLSTM_HEAD_DIM
    out_shape = (jax.ShapeDtypeStruct((bsz, seq, D_MODEL), F32),
                 jax.ShapeDtypeStruct((bsz, CONV_WIDTH - 1, D_INNER), F32),
                 jax.ShapeDtypeStruct((bsz, MLSTM_HEADS, hd_dim, hd_dim), F32),
                 jax.ShapeDtypeStruct((bsz, 1, D_INNER), F32),
                 jax.ShapeDtypeStruct((bsz, 1, LANES), F32))
    out_specs = (pl.BlockSpec((1, tb, D_MODEL), lambda b, t: (b, t, 0)),
                 pl.BlockSpec((1, CONV_WIDTH - 1, D_INNER), lambda b, t: (b, 0, 0)),
                 pl.BlockSpec((1, MLSTM_HEADS, hd_dim, hd_dim), lambda b, t: (b, 0, 0, 0)),
                 pl.BlockSpec((1, 1, D_INNER), lambda b, t: (b, 0, 0)),
                 pl.BlockSpec((1, 1, LANES), lambda b, t: (b, 0, 0)))
    scratch = [pltpu.VMEM((D_MODEL // LANES, tb, LANES), F32),
               pltpu.VMEM((tb, D_MODEL), BF16),
               pltpu.VMEM((MLSTM_HEADS, WRAP, hd_dim), F32),
               pltpu.VMEM((tb, D_INNER), F32),
               pltpu.VMEM((tb, 3 * D_INNER), BF16),
               pltpu.VMEM((tb, D_INNER), BF16),
               pltpu.VMEM((tb, D_INNER), F32),
               pltpu.VMEM((tb, D_INNER), BF16),
               pltpu.VMEM((SUBLANES, D_INNER), F32),
               pltpu.VMEM((SUBLANES, LANES), F32)]
    scratch += [pltpu.VMEM((tb // SSD_CHUNK, WRAP + SSD_CHUNK, hd_dim), F32) for _ in range(MLSTM_HEADS)]
    return pl.pallas_call(
        _mlstm_prompt_kernel, grid=grid, in_specs=in_specs, out_specs=out_specs,
        out_shape=out_shape, scratch_shapes=scratch, name="mlstm_prompt",
        compiler_params=pltpu.CompilerParams(
            dimension_semantics=("arbitrary", "arbitrary"), vmem_limit_bytes=VMEM_LIMIT),
    )(x, *consts)


N_SAMPLE = 128
S0_SEQS = 4


def _split3(v):
    hi = v.astype(BF16)
    r1 = v - hi.astype(F32)
    mid = r1.astype(BF16)
    lo = (r1 - mid.astype(F32)).astype(BF16)
    return hi, mid, lo


def _expand_heads(v, width):
    rr = lax.broadcasted_iota(jnp.int32, (3 * LANES, D_INNER), 0) & (LANES - 1)
    cc = lax.broadcasted_iota(jnp.int32, (3 * LANES, D_INNER), 1) // width
    e3 = jnp.where(rr == cc, 1.0, 0.0).astype(BF16)
    return _dot(jnp.concatenate(_split3(v), axis=1), e3)


def _ssd_sample_pre_kernel(x_ref, cs_ref, prew_ref, win_ref, wdt_ref, convw_ref, convb_ref,
                           dtb_ref, alog_ref,
                           z_ref, xact_ref, b_ref, ct_ref, xdtt_ref, dec_ref, csnew_ref):
    cd = SSD_CONV_DIM
    wz_ref, wxbc_ref = _ssd_win_views(win_ref)
    u = _rms(x_ref[...], prew_ref[...]).astype(BF16)
    xbc = _dot(u, wxbc_ref[...])
    acc = convb_ref[...] + xbc * convw_ref[CONV_WIDTH - 1:CONV_WIDTH, :]
    for k in range(CONV_WIDTH - 1):
        acc = acc + cs_ref[:, k * cd:(k + 1) * cd] * convw_ref[k:k + 1, :]
    act = _silu(acc)
    csnew_ref[:, 0:2 * cd] = cs_ref[:, cd:3 * cd]
    csnew_ref[:, 2 * cd:3 * cd] = xbc
    xact = act[:, :D_INNER]
    dt = _softplus(_dot(u, wdt_ref[...].astype(BF16)) + dtb_ref[...])
    dec_ref[...] = jnp.exp(dt * (-jnp.exp(alog_ref[...])))
    xdt = xact * _expand_heads(dt, SSD_HEAD_DIM)
    xdtt_ref[...] = xdt.T.astype(BF16)
    z_ref[...] = _dot(u, wz_ref[...])
    xact_ref[...] = xact
    b_ref[...] = act[:, D_INNER:D_INNER + SSD_GN]
    ct_ref[...] = act[:, D_INNER + SSD_GN:].T


def _ssd_sample_pre(x, cs, prew, win, wdt, convw, convb, dtb, alog):
    n = x.shape[0]
    out_shape = (jax.ShapeDtypeStruct((n, D_INNER), F32),
                 jax.ShapeDtypeStruct((n, D_INNER), F32),
                 jax.ShapeDtypeStruct((n, SSD_GN), F32),
                 jax.ShapeDtypeStruct((SSD_GN, n), F32),
                 jax.ShapeDtypeStruct((D_INNER, n), BF16),
                 jax.ShapeDtypeStruct((n, LANES), F32),
                 jax.ShapeDtypeStruct((n, (CONV_WIDTH - 1) * SSD_CONV_DIM), F32))
    return pl.pallas_call(
        _ssd_sample_pre_kernel, out_shape=out_shape, name="ssd_sample_pre",
        compiler_params=pltpu.CompilerParams(vmem_limit_bytes=VMEM_LIMIT),
    )(x, cs, prew, win, wdt, convw, convb, dtb, alog)


def _ssd_sample_state_kernel(dec_ref, h_ref, xdtt_ref, b_ref, ct_ref, hnew_ref, yt_ref):
    j = pl.program_id(0)
    gw = D_INNER // SSD_GROUPS
    hpg = SSD_HEADS // SSD_GROUPS

    @pl.when(j == 0)
    def _():
        yt_ref[...] = jnp.zeros_like(yt_ref)

    rowid = lax.broadcasted_iota(jnp.int32, (N_SAMPLE, SSD_STATE), 0)
    colid = lax.broadcasted_iota(jnp.int32, (SSD_STATE, N_SAMPLE), 1)
    for pp in range(S0_SEQS // 2):
        seqs = (j * S0_SEQS + 2 * pp, j * S0_SEQS + 2 * pp + 1)
        for g in range(SSD_GROUPS):
            bg = b_ref[:, g * SSD_STATE:(g + 1) * SSD_STATE]
            ctg = ct_ref[g * SSD_STATE:(g + 1) * SSD_STATE, :]
            rb = jnp.concatenate([jnp.where(rowid == s, bg, 0.0) for s in seqs], axis=1).astype(BF16)
            upd = _dot(xdtt_ref[g * gw:(g + 1) * gw, :], rb)
            for i, s in enumerate(seqs):
                for r in range(hpg):
                    rows = slice(g * gw + r * SSD_HEAD_DIM, g * gw + (r + 1) * SSD_HEAD_DIM)
                    dec = dec_ref[s * SSD_HEADS + g * hpg + r]
                    hn = (h_ref[2 * pp + i, rows, :] * dec
                          + upd[r * SSD_HEAD_DIM:(r + 1) * SSD_HEAD_DIM, i * SSD_STATE:(i + 1) * SSD_STATE])
                    hnew_ref[2 * pp + i, rows, :] = hn
            hn_pair = jnp.concatenate(
                [hnew_ref[2 * pp + i, g * gw:(g + 1) * gw, :] for i in range(2)], axis=1).astype(BF16)
            wc = jnp.concatenate([jnp.where(colid == s, ctg, 0.0) for s in seqs], axis=0).astype(BF16)
            yt_ref[g * gw:(g + 1) * gw, :] += _dot(hn_pair, wc)


def _ssd_sample_state(dec_flat, h, xdtt, b, ct):
    n = h.shape[0]
    grid = (n // S0_SEQS,)
    in_specs = [pl.BlockSpec(memory_space=pltpu.SMEM),
                pl.BlockSpec((S0_SEQS, D_INNER, SSD_STATE), lambda j: (j, 0, 0)),
                _const_spec(xdtt.shape), _const_spec(b.shape), _const_spec(ct.shape)]
    out_shape = (jax.ShapeDtypeStruct(h.shape, F32),
                 jax.ShapeDtypeStruct((D_INNER, n), F32))
    out_specs = (pl.BlockSpec((S0_SEQS, D_INNER, SSD_STATE), lambda j: (j, 0, 0)),
                 pl.BlockSpec((D_INNER, n), lambda j: (0, 0)))
    return pl.pallas_call(
        _ssd_sample_state_kernel, grid=grid, in_specs=in_specs, out_specs=out_specs,
        out_shape=out_shape, name="ssd_sample_state",
        compiler_params=pltpu.CompilerParams(
            dimension_semantics=("arbitrary",), vmem_limit_bytes=VMEM_LIMIT),
    )(dec_flat, h, xdtt, b, ct)


def _head_cols(a, hd):
    return a[:, hd * MLSTM_HEAD_DIM:(hd + 1) * MLSTM_HEAD_DIM]


def _sample_mid_kernel(x_ref, yt_ref, xact_ref, z_ref, dexp_ref, normw_ref, wout0_ref, postw0_ref,
                       mcs_ref, n0_ref, m0_ref, prew1_ref, win1_ref, convw_ref,
                       convb_ref, wq_ref, wk_ref, wv_ref, wg_ref, bg_ref,
                       x1_ref, z1_ref, o1_ref, xc_ref, v_ref, kt_ref, wv_out_ref, qs_ref,
                       s_ref, iw_ref, hden_ref, mcsnew_ref, nnew_ref, mnew_ref):
    gw = D_INNER // SSD_GROUPS
    wxm_ref, wz1_ref, wo1_ref = _mlstm_win_views(win1_ref)
    y = yt_ref[...].T + dexp_ref[...] * xact_ref[...]
    y = y * _silu(z_ref[...])
    parts = []
    for g in range(SSD_GROUPS):
        cols = slice(g * gw, (g + 1) * gw)
        parts.append(_rms(y[:, cols], normw_ref[:, cols]).astype(BF16))
    out0 = _dot(jnp.concatenate(parts, axis=1), wout0_ref[...])
    x1 = x_ref[...] + _rms(out0, postw0_ref[...])
    x1_ref[...] = x1

    u = _rms(x1, prew1_ref[...]).astype(BF16)
    xm = _dot(u, wxm_ref[...])
    z1_ref[...] = _dot(u, wz1_ref[...])
    o1_ref[...] = _dot(u, wo1_ref[...])
    acc = convb_ref[...] + xm * convw_ref[CONV_WIDTH - 1:CONV_WIDTH, :]
    for k in range(CONV_WIDTH - 1):
        acc = acc + mcs_ref[:, k * D_INNER:(k + 1) * D_INNER] * convw_ref[k:k + 1, :]
    xc = _silu(acc)
    xc_ref[...] = xc
    mcsnew_ref[:, 0:2 * D_INNER] = mcs_ref[:, D_INNER:3 * D_INNER]
    mcsnew_ref[:, 2 * D_INNER:3 * D_INNER] = xm

    scale = MLSTM_HEAD_DIM ** -0.5
    qs, ks, vs = [], [], []
    for hd in range(MLSTM_HEADS):
        xc_h = _head_cols(xc, hd).astype(BF16)
        xm_h = _head_cols(xm, hd).astype(BF16)
        qs.append(_dot(xc_h, wq_ref[hd]))
        ks.append(_dot(xc_h, wk_ref[hd]))
        vs.append(_dot(xm_h, wv_ref[hd]))
    qkv = jnp.concatenate(qs + ks + vs, axis=1).astype(BF16)
    gates = _dot(qkv, wg_ref[...]) + bg_ref[...]
    li = gates
    lf = pltpu.roll(_log_sigmoid(gates), LANES - MLSTM_HEADS, axis=1)
    m0 = m0_ref[...]
    inter = lf + m0
    m_t = jnp.maximum(inter, li)
    w = jnp.exp(li - m_t)
    iw = jnp.exp(inter - m_t)
    mnew_ref[...] = m_t

    lane = lax.broadcasted_iota(jnp.int32, (N_SAMPLE, LANES), 1)
    qk_arr = jnp.zeros((N_SAMPLE, LANES), F32)
    qn_arr = jnp.zeros((N_SAMPLE, LANES), F32)
    kb_parts = []
    for hd in range(MLSTM_HEADS):
        cols = slice(hd * MLSTM_HEAD_DIM, (hd + 1) * MLSTM_HEAD_DIM)
        q_h = (qs[hd] * scale).astype(BF16).astype(F32)
        k_h = ks[hd].astype(BF16).astype(F32)
        n_h = n0_ref[:, cols]
        qk_arr = jnp.where(lane == hd, jnp.sum(q_h * k_h, axis=1, keepdims=True), qk_arr)
        qn_arr = jnp.where(lane == hd, jnp.sum(q_h * n_h, axis=1, keepdims=True), qn_arr)
        w_h = w[:, hd:hd + 1]
        iw_h = iw[:, hd:hd + 1]
        nnew_ref[:, cols] = iw_h * n_h + w_h * k_h
        wv_out_ref[:, cols] = w_h * vs[hd]
        v_ref[:, cols] = vs[hd]
        qs_ref[:, cols] = q_h
        kb_parts.append(k_h)
    kt_ref[...] = jnp.concatenate(kb_parts, axis=1).T.astype(BF16)
    s = qk_arr * w
    den = s + iw * qn_arr
    s_ref[...] = s
    iw_ref[...] = iw
    hden_ref[...] = jnp.maximum(jnp.abs(den), jnp.exp(-m_t))


def _sample_mid(*args):
    n = N_SAMPLE
    wide = jax.ShapeDtypeStruct((n, D_INNER), F32)
    small = jax.ShapeDtypeStruct((n, LANES), F32)
    out_shape = (jax.ShapeDtypeStruct((n, D_MODEL), F32),
                 wide, wide, wide, wide,
                 jax.ShapeDtypeStruct((D_INNER, n), BF16),
                 wide, wide,
                 small, small, small,
                 jax.ShapeDtypeStruct((n, (CONV_WIDTH - 1) * D_INNER), F32),
                 wide, small)
    return pl.pallas_call(
        _sample_mid_kernel, out_shape=out_shape, name="sample_mid",
        compiler_params=pltpu.CompilerParams(vmem_limit_bytes=VMEM_LIMIT),
    )(*args)


def _mlstm_sample_post_kernel(x1_ref, num_ref, v_ref, s_ref, iw_ref, hden_ref, o1_ref, z1_ref,
                              xc_ref, hnw_ref, skip_ref, wout_ref, postw_ref, y_ref):
    parts = []
    for hd in range(MLSTM_HEADS):
        cols = slice(hd * MLSTM_HEAD_DIM, (hd + 1) * MLSTM_HEAD_DIM)
        num = s_ref[:, hd:hd + 1] * v_ref[:, cols] + iw_ref[:, hd:hd + 1] * num_ref[:, cols]
        hh = num / hden_ref[:, hd:hd + 1]
        hc = hh - jnp.mean(hh, axis=1, keepdims=True)
        hn = hc * lax.rsqrt(jnp.mean(hc * hc, axis=1, keepdims=True) + EPS)
        hg = _sigmoid(o1_ref[:, cols]) * (hn * hnw_ref[:, cols]) + skip_ref[:, cols] * xc_ref[:, cols]
        parts.append((hg * _silu(z1_ref[:, cols])).astype(BF16))
    out = _dot(jnp.concatenate(parts, axis=1), wout_ref[...])
    y_ref[...] = x1_ref[...] + _rms(out, postw_ref[...])


def _mlstm_sample_post(*args):
    return pl.pallas_call(
        _mlstm_sample_post_kernel, out_shape=jax.ShapeDtypeStruct((N_SAMPLE, D_MODEL), F32),
        name="mlstm_sample_post",
        compiler_params=pltpu.CompilerParams(vmem_limit_bytes=VMEM_LIMIT),
    )(*args)


def kernel(x_prompt, x_sample, state_ssd_conv, state_ssd, state_mlstm_conv, state_mlstm_c, state_mlstm_n, state_mlstm_m, pre_norm_w, post_norm_w, ssd_w_in, ssd_conv_w, ssd_conv_b, ssd_dt_bias, ssd_a_log, ssd_d, ssd_norm_w, ssd_w_out, ml_w_in, ml_conv_w, ml_conv_b, ml_w_q, ml_w_k, ml_w_v, ml_w_gate, ml_b_gate, ml_head_norm_w, ml_skip, ml_w_out):
    win0 = ssd_w_in[0][:, :D_INNER + SSD_CONV_DIM].astype(BF16)
    wdt = _pad_lanes(ssd_w_in[0][:, D_INNER + SSD_CONV_DIM:])
    dtb = _pad_lanes(ssd_dt_bias[0][None])
    alog = _pad_lanes(ssd_a_log[0][None])
    dexp = jnp.repeat(ssd_d[0], SSD_HEAD_DIM)[None]
    prew0, postw0 = pre_norm_w[0][None], post_norm_w[0][None]
    prew1, postw1 = pre_norm_w[1][None], post_norm_w[1][None]
    convw0, convb0 = ssd_conv_w[0], ssd_conv_b[0][None]
    normw0 = ssd_norm_w[0][None]
    wout0 = ssd_w_out[0].astype(BF16)
    win1 = ml_w_in[0].astype(BF16)
    wg = _pad_lanes(ml_w_gate[0]).astype(BF16)
    bg = _pad_lanes(ml_b_gate[0][None])
    convw1, convb1 = ml_conv_w[0], ml_conv_b[0][None]
    wq, wk, wv = ml_w_q[0].astype(BF16), ml_w_k[0].astype(BF16), ml_w_v[0].astype(BF16)
    hnw, skip = ml_head_norm_w[0][None], ml_skip[0][None]
    wout1 = ml_w_out[0].astype(BF16)

    n = x_sample.shape[0]
    xs = x_sample.reshape(n, D_MODEL)
    cs0 = state_ssd_conv[0].reshape(n, (CONV_WIDTH - 1) * SSD_CONV_DIM)
    z0, xact0, b0, ct0, xdtt0, dec0, csnew0 = _ssd_sample_pre(
        xs, cs0, prew0, win0, wdt, convw0, convb0, dtb, alog)
    h0 = state_ssd[0].reshape(n, D_INNER, SSD_STATE)
    hnew0, yt0 = _ssd_sample_state(dec0[:, :SSD_HEADS].reshape(-1), h0, xdtt0, b0, ct0)
    mcs = state_mlstm_conv[0].reshape(n, (CONV_WIDTH - 1) * D_INNER)
    n0 = state_mlstm_n[0].reshape(n, D_INNER)
    m0 = _pad_lanes(state_mlstm_m[0])
    (x1, z1, o1, xc1, v1, kt1, wv1, qs1, s1, iw1, hden1, mcsnew, nnew, mnew) = _sample_mid(
        xs, yt0, xact0, z0, dexp, normw0, wout0, postw0,
        mcs, n0, m0, prew1, win1, convw1, convb1, wq, wk, wv, wg, bg)

    bp = x_prompt.shape[0]
    y0, p_sc, p_sh, cnew, num1 = _ssd_prompt(
        x_prompt, prew0, win0, wdt, convw0, convb0, dtb, alog, dexp, normw0, wout0, postw0,
        iw1[:, :MLSTM_HEADS].reshape(-1), state_mlstm_c[0], kt1, wv1, qs1)
    y_prompt, p_mc, p_c, p_n, p_m = _mlstm_prompt(
        y0, prew1, win1, convw1, convb1, wq, wk, wv, wg, bg, hnw, skip, wout1, postw1)
    p_sc = p_sc[None]
    p_sh = p_sh.reshape(1, bp, SSD_HEADS, SSD_HEAD_DIM, SSD_STATE)
    p_mc = p_mc[None]
    p_c = p_c[None]
    p_n = p_n.reshape(1, bp, MLSTM_HEADS, MLSTM_HEAD_DIM)
    p_m = p_m[:, 0, :MLSTM_HEADS][None]

    y_s = _mlstm_sample_post(x1, num1, v1, s1, iw1, hden1, o1, z1, xc1, hnw, skip, wout1, postw1)

    y_sample = y_s.reshape(n, 1, D_MODEL)
    s_sc = csnew0.reshape(1, n, CONV_WIDTH - 1, SSD_CONV_DIM)
    s_sh = hnew0.reshape(1, n, SSD_HEADS, SSD_HEAD_DIM, SSD_STATE)
    s_mc = mcsnew.reshape(1, n, CONV_WIDTH - 1, D_INNER)
    s_c = cnew[None]
    s_n = nnew.reshape(1, n, MLSTM_HEADS, MLSTM_HEAD_DIM)
    s_m = mnew[:, :MLSTM_HEADS][None]
    return (y_prompt, y_sample, p_sc, s_sc, p_sh, s_sh, p_mc, s_mc, p_c, s_c, p_n, s_n, p_m, s_m)
```

```python
import jax
import jax.numpy as jnp
from jax import lax
from jax.experimental import pallas as pl
from jax.experimental.pallas import tpu as pltpu

F32 = jnp.float32
BF16 = jnp.bfloat16

D_MODEL = 1024
D_INNER = 2048
CONV_WIDTH = 4
SSD_HEAD_DIM = 64
SSD_HEADS = 32
SSD_GROUPS = 4
SSD_STATE = 128
SSD_GN = SSD_GROUPS * SSD_STATE
SSD_CONV_DIM = D_INNER + 2 * SSD_GN
SSD_CHUNK = 128
MLSTM_HEADS = 4
MLSTM_HEAD_DIM = 512
EPS = 1e-6

LANES = 128
SUBLANES = 8
HALO = SUBLANES
VMEM_LIMIT = 58 * 1024 * 1024

L0_TB = 256


def _sigmoid(x):
    return 0.5 * jnp.tanh(0.5 * x) + 0.5


def _silu(x):
    h = 0.5 * x
    return h * jnp.tanh(h) + h


def _softplus(x):
    return jnp.maximum(x, 0.0) + jnp.log1p(jnp.exp(-jnp.abs(x)))


def _rms(x, w):
    ms = jnp.mean(x * x, axis=-1, keepdims=True)
    return x * lax.rsqrt(ms + EPS) * w


def _dot(a, b):
    return jnp.dot(a, b, preferred_element_type=F32)


def _dot_exact(a, b):
    return jnp.dot(a, b, preferred_element_type=F32, precision=lax.Precision.HIGHEST)


def _tri(n):
    r = lax.broadcasted_iota(jnp.int32, (n, n), 0)
    c = lax.broadcasted_iota(jnp.int32, (n, n), 1)
    return r >= c


WRAP = (CONV_WIDTH - 1) * SUBLANES
CONV_TILE = 512
LOG2E = 1.4426950408889634


def _stage_columns(x_ref, xs_scr):
    for k in range(xs_scr.shape[0]):
        xs_scr[k] = x_ref[0, :, k * LANES:(k + 1) * LANES]


def _load_permuted(xs_scr, tb, q):
    nv = q // SUBLANES
    cols = []
    for k in range(xs_scr.shape[0]):
        parts = [xs_scr[k, pl.ds(c * q + j, SUBLANES, stride=nv), :]
                 for c in range(tb // q) for j in range(nv)]
        cols.append(jnp.concatenate(parts, axis=0))
    return jnp.concatenate(cols, axis=1)


def _store_unpermuted(y_ref, xs_scr, y, tb, q):
    nv = q // SUBLANES
    for k in range(xs_scr.shape[0]):
        for c in range(tb // q):
            for j in range(nv):
                r0 = c * q + j * SUBLANES
                xs_scr[k, pl.ds(c * q + j, SUBLANES, stride=nv), :] = y[r0:r0 + SUBLANES,
                                                                        k * LANES:(k + 1) * LANES]
    for k in range(xs_scr.shape[0]):
        y_ref[0, :, k * LANES:(k + 1) * LANES] = xs_scr[k]


def _perm_causal(n, q):
    nv = q // SUBLANES
    r = lax.broadcasted_iota(jnp.int32, (n, n), 0)
    c = lax.broadcasted_iota(jnp.int32, (n, n), 1)
    tok = lambda p: (p & -q) + (p & (SUBLANES - 1)) * nv + ((p & (q - 1)) >> 3)
    return tok(r) >= tok(c)


def _fill_wrap_rows(xe_t, hist_t, nchunk, q):
    sub = lax.broadcasted_iota(jnp.int32, (SUBLANES, xe_t.shape[-1]), 0)
    for c in range(nchunk):
        for i in range(CONV_WIDTH - 1):
            src = slice(q + i * SUBLANES, q + (i + 1) * SUBLANES)
            cur = xe_t[c, src, :]
            prev = hist_t[i * SUBLANES:(i + 1) * SUBLANES, :] if c == 0 else xe_t[c - 1, src, :]
            xe_t[c, i * SUBLANES:(i + 1) * SUBLANES, :] = jnp.where(
                sub == 0, pltpu.roll(prev, 1, axis=0), pltpu.roll(cur, 1, axis=0))
    hist_t[...] = xe_t[nchunk - 1, q:q + WRAP, :]


def _conv_tile(xe_t, convw_ref, convb_ref, cols, c, q):
    acc = convb_ref[:, cols]
    for k in range(CONV_WIDTH):
        acc = acc + xe_t[c, k * SUBLANES:k * SUBLANES + q, :] * convw_ref[k:k + 1, cols]
    return acc


def _store_conv_state(convst_ref, b, xe_scr, nchunk, q, tile):
    for n in range(len(xe_scr)):
        for i in range(CONV_WIDTH - 1):
            row = q + (i + 1) * SUBLANES - 1
            convst_ref[i, pl.ds(b, 1), n * tile:(n + 1) * tile] = xe_scr[n][nchunk - 1, row:row + 1, :]


def _ssd_win_views(win_ref):
    return win_ref.at[:, 0:D_INNER], win_ref.at[:, D_INNER:D_INNER + SSD_CONV_DIM]


def _mlstm_win_views(win_ref):
    return (win_ref.at[:, 0:D_INNER], win_ref.at[:, D_INNER:2 * D_INNER],
            win_ref.at[:, 2 * D_INNER:3 * D_INNER])


def _const_spec(shape):
    nd = len(shape)
    return pl.BlockSpec(shape, lambda *_: (0,) * nd, pipeline_mode=pl.Buffered(1))


STREAM_SEQS = 2


class _MlstmStateStream:
    def __init__(self, step, nsteps, cd_ref, c_hbm, kt_ref, wv_ref, qs_ref, cnew_hbm, num_ref,
                 cin_scr, cout_scr, sem_in, sem_out):
        self.step, self.nsteps = step, nsteps
        self.cd_ref, self.kt_ref, self.wv_ref, self.qs_ref = cd_ref, kt_ref, wv_ref, qs_ref
        self.num_ref, self.cin_scr, self.cout_scr = num_ref, cin_scr, cout_scr
        self.in_copy = lambda s: pltpu.make_async_copy(
            c_hbm.at[pl.ds(s * STREAM_SEQS, STREAM_SEQS)], cin_scr, sem_in)
        self.out_copy = lambda s: pltpu.make_async_copy(
            cout_scr, cnew_hbm.at[pl.ds(s * STREAM_SEQS, STREAM_SEQS)], sem_out)

    def begin(self):
        step = self.step

        @pl.when(step == 0)
        def _():
            self.in_copy(0).start()

        self.in_copy(step).wait()

        @pl.when(step > 0)
        def _():
            self.out_copy(step - 1).wait()

    def update(self, i, hd):
        b = self.step * STREAM_SEQS + i
        b8 = pl.multiple_of((b // SUBLANES) * SUBLANES, SUBLANES)
        rowid = lax.broadcasted_iota(jnp.int32, (N_SAMPLE, MLSTM_HEAD_DIM), 0)
        sub = lax.broadcasted_iota(jnp.int32, (SUBLANES, MLSTM_HEAD_DIM), 0)
        cols = slice(hd * MLSTM_HEAD_DIM, (hd + 1) * MLSTM_HEAD_DIM)
        c_old = self.cin_scr[i, hd]
        rv = jnp.where(rowid == b, self.wv_ref[:, cols], 0.0).astype(BF16)
        upd = _dot(self.kt_ref[cols, :], rv)
        self.cout_scr[i, hd] = self.cd_ref[b * MLSTM_HEADS + hd] * c_old + upd
        q8 = self.qs_ref[pl.ds(b8, SUBLANES), cols].astype(BF16)
        r8 = _dot(q8, c_old.astype(BF16))
        self.num_ref[pl.ds(b, 1), cols] = jnp.sum(jnp.where(sub == b - b8, r8, 0.0), axis=0, keepdims=True)

    def end(self):
        step, nsteps = self.step, self.nsteps
        self.out_copy(step).start()

        @pl.when(step + 1 < nsteps)
        def _():
            self.in_copy(step + 1).start()

        @pl.when(step == nsteps - 1)
        def _():
            self.out_copy(step).wait()


def _ssd_prompt_kernel(cd_ref, x_ref, prew_ref, win_ref, wdt_ref, convw_ref, convb_ref,
                       dtb_ref, alog_ref, dexp_ref, normw_ref, wout_ref, postw_ref, e3_ref,
                       c_hbm, kt_ref, wv_ref, qs_ref,
                       y_ref, convst_ref, hst_ref, cnew_hbm, num_ref,
                       xs_scr, u_scr, hist_scr, xact_scr, b_scr, c_scr, dt_scr, da_scr, yb_scr,
                       st_scr, cin_scr, cout_scr, sem_in, sem_out, *xe_scr):
    tb = L0_TB
    t = pl.program_id(1)
    nt = pl.num_programs(1)
    wz_ref, wxbc_ref = _ssd_win_views(win_ref)

    stream = _MlstmStateStream(pl.program_id(0) * nt + t, pl.num_programs(0) * nt,
                               cd_ref, c_hbm, kt_ref, wv_ref, qs_ref, cnew_hbm, num_ref,
                               cin_scr, cout_scr, sem_in, sem_out)
    stream.begin()
    for i in range(STREAM_SEQS):
        for hd in range(MLSTM_HEADS):
            stream.update(i, hd)
    stream.end()

    q = SSD_CHUNK
    nchunk = tb // q

    @pl.when(t == 0)
    def _():
        hist_scr[...] = jnp.zeros_like(hist_scr)
        st_scr[...] = jnp.zeros_like(st_scr)

    _stage_columns(x_ref, xs_scr)
    x = _load_permuted(xs_scr, tb, q)
    u = _rms(x, prew_ref[...]).astype(BF16)
    u_scr[...] = u

    for n in range(SSD_CONV_DIM // CONV_TILE):
        res = _dot(u, wxbc_ref[:, n * CONV_TILE:(n + 1) * CONV_TILE])
        for c in range(nchunk):
            xe_scr[n][c, WRAP:WRAP + q, :] = res[c * q:(c + 1) * q]
    for n in range(SSD_CONV_DIM // CONV_TILE):
        n0 = n * CONV_TILE
        cols = slice(n0, n0 + CONV_TILE)
        xe_t = xe_scr[n]
        _fill_wrap_rows(xe_t, hist_scr.at[n], nchunk, q)
        for c in range(nchunk):
            rows = slice(c * q, (c + 1) * q)
            act = _silu(_conv_tile(xe_t, convw_ref, convb_ref, cols, c, q))
            if n0 < D_INNER:
                xact_scr[rows, cols] = act
            elif n0 < D_INNER + SSD_GN:
                b_scr[rows, n0 - D_INNER:n0 - D_INNER + CONV_TILE] = act
            else:
                c_scr[rows, n0 - D_INNER - SSD_GN:n0 - D_INNER - SSD_GN + CONV_TILE] = act

    @pl.when(t == nt - 1)
    def _():
        _store_conv_state(convst_ref, pl.program_id(0), xe_scr, nchunk, q, CONV_TILE)

    dt = _softplus(_dot(u, wdt_ref[...].astype(BF16)) + dtb_ref[...])
    dt_scr[...] = dt
    da_scr[...] = dt * (-jnp.exp(alog_ref[...])) * LOG2E

    causal = _perm_causal(q, q)
    tri_f = causal.astype(F32)
    lane = lax.broadcasted_iota(jnp.int32, (q, LANES), 1)
    lane_lo = lane < SSD_HEAD_DIM
    hpg = SSD_HEADS // SSD_GROUPS
    gw = D_INNER // SSD_GROUPS

    for ci in range(nchunk):
        rows = slice(ci * q, (ci + 1) * q)
        cum = _dot_exact(tri_f, da_scr[rows, :])
        cum_t = cum.T
        dt_e = _dot(jnp.concatenate(_split3(dt_scr[rows, :]), axis=1), e3_ref[...])
        for g in range(SSD_GROUPS):
            bg = b_scr[rows, g * SSD_STATE:(g + 1) * SSD_STATE]
            cg = c_scr[rows, g * SSD_STATE:(g + 1) * SSD_STATE].astype(BF16)
            bg_t = bg.T.astype(BF16)
            cb = _dot(cg, bg_t)
            st_g = st_scr[:, g * gw:(g + 1) * gw]
            y_inter = _dot(cg, st_g.astype(BF16))
            xt_parts = []
            el_parts = []
            for jj in range(hpg // 2):
                j = g * (hpg // 2) + jj
                h0, h1 = 2 * j, 2 * j + 1
                col0 = jnp.broadcast_to(cum[:, h0:h0 + 1], (q, q))
                col1 = jnp.broadcast_to(cum[:, h1:h1 + 1], (q, q))
                row0 = jnp.broadcast_to(cum_t[h0:h0 + 1, :], (q, q))
                row1 = jnp.broadcast_to(cum_t[h1:h1 + 1, :], (q, q))
                dec0 = jnp.exp2(jnp.where(causal, col0 - row0, -jnp.inf))
                dec1 = jnp.exp2(jnp.where(causal, col1 - row1, -jnp.inf))
                m_pair = jnp.concatenate([(cb * dec0).astype(BF16), (cb * dec1).astype(BF16)], axis=1)
                col_p = jnp.where(lane_lo, col0, col1)
                cols = slice(j * LANES, (j + 1) * LANES)
                dt_p = dt_e[:, cols]
                xp = xact_scr[rows, cols]
                xdt = xp * dt_p
                x_bd = jnp.concatenate([jnp.where(lane_lo, xdt, 0.0).astype(BF16),
                                        jnp.where(lane_lo, 0.0, xdt).astype(BF16)], axis=0)
                y_p = _dot(m_pair, x_bd)
                y_p = y_p + y_inter[:, jj * LANES:(jj + 1) * LANES] * jnp.exp2(col_p)
                y_p = y_p + dexp_ref[:, cols] * xp
                yb_scr[rows, cols] = y_p
                last_p = col_p[q - 1:q, :]
                xt_parts.append((xdt * jnp.exp2(last_p - col_p)).astype(BF16))
                el_parts.append(jnp.exp2(last_p))
            xt = jnp.concatenate(xt_parts, axis=1)
            el = jnp.concatenate(el_parts, axis=1)
            st_scr[:, g * gw:(g + 1) * gw] = st_g * el + _dot(bg_t, xt)

    @pl.when(t == nt - 1)
    def _():
        for k in range(D_INNER // LANES):
            hst_ref[0, k * LANES:(k + 1) * LANES, :] = st_scr[:, k * LANES:(k + 1) * LANES].T

    for g in range(SSD_GROUPS):
        cols = slice(g * gw, (g + 1) * gw)
        z = _dot(u_scr[...], wz_ref[:, cols])
        yg = yb_scr[:, cols] * _silu(z)
        yg = _rms(yg, normw_ref[:, cols])
        yb_scr[:, cols] = yg
    out = _dot(yb_scr[...].astype(BF16), wout_ref[...])
    y_ref[0] = x + _rms(out, postw_ref[...])


def _ssd_prompt(x, prew, win, wdt, convw, convb, dtb, alog, dexp, normw, wout, postw, e3,
                cd_flat, c_state, kt, wv, qs):
    bsz, seq, _ = x.shape
    tb = L0_TB
    grid = (bsz, seq // tb)
    n_sample = c_state.shape[0]
    assert n_sample == grid[0] * grid[1] * STREAM_SEQS
    consts = (prew, win, wdt, convw, convb, dtb, alog, dexp, normw, wout, postw, e3)
    in_specs = [pl.BlockSpec(memory_space=pltpu.SMEM),
                pl.BlockSpec((1, tb, D_MODEL), lambda b, t: (b, t, 0))]
    in_specs += [_const_spec(c.shape) for c in consts]
    in_specs += [pl.BlockSpec(memory_space=pl.ANY),
                 _const_spec(kt.shape), _const_spec(wv.shape), _const_spec(qs.shape)]
    out_shape = (jax.ShapeDtypeStruct((bsz, seq, D_MODEL), F32),
                 jax.ShapeDtypeStruct((CONV_WIDTH - 1, bsz, SSD_CONV_DIM), F32),
                 jax.ShapeDtypeStruct((bsz, D_INNER, SSD_STATE), F32),
                 jax.ShapeDtypeStruct(c_state.shape, F32),
                 jax.ShapeDtypeStruct((n_sample, D_INNER), F32))
    out_specs = (pl.BlockSpec((1, tb, D_MODEL), lambda b, t: (b, t, 0)),
                 pl.BlockSpec((CONV_WIDTH - 1, bsz, SSD_CONV_DIM), lambda b, t: (0, 0, 0)),
                 pl.BlockSpec((1, D_INNER, SSD_STATE), lambda b, t: (b, 0, 0)),
                 pl.BlockSpec(memory_space=pl.ANY),
                 pl.BlockSpec((n_sample, D_INNER), lambda b, t: (0, 0)))
    stream_blk = (STREAM_SEQS, MLSTM_HEADS, MLSTM_HEAD_DIM, MLSTM_HEAD_DIM)
    scratch = [pltpu.VMEM((D_MODEL // LANES, tb, LANES), F32),
               pltpu.VMEM((tb, D_MODEL), BF16),
               pltpu.VMEM((SSD_CONV_DIM // CONV_TILE, WRAP, CONV_TILE), F32),
               pltpu.VMEM((tb, D_INNER), F32),
               pltpu.VMEM((tb, SSD_GN), F32),
               pltpu.VMEM((tb, SSD_GN), F32),
               pltpu.VMEM((tb, LANES), F32),
               pltpu.VMEM((tb, LANES), F32),
               pltpu.VMEM((tb, D_INNER), F32),
               pltpu.VMEM((SSD_STATE, D_INNER), F32),
               pltpu.VMEM(stream_blk, F32),
               pltpu.VMEM(stream_blk, F32),
               pltpu.SemaphoreType.DMA(()),
               pltpu.SemaphoreType.DMA(())]
    scratch += [pltpu.VMEM((tb // SSD_CHUNK, WRAP + SSD_CHUNK, CONV_TILE), F32)
                for _ in range(SSD_CONV_DIM // CONV_TILE)]
    return pl.pallas_call(
        _ssd_prompt_kernel, grid=grid, in_specs=in_specs, out_specs=out_specs,
        out_shape=out_shape, scratch_shapes=scratch, name="ssd_prompt",
        compiler_params=pltpu.CompilerParams(
            dimension_semantics=("arbitrary", "arbitrary"), vmem_limit_bytes=VMEM_LIMIT),
    )(cd_flat, x, *consts, c_state, kt, wv, qs)


def _pad_lanes(a, n=LANES):
    return jnp.pad(a, ((0, 0), (0, n - a.shape[-1])))


def _cast_ssd_w_in_kernel(wt_ref, wt_dt_ref, win_ref, wdt_ref, pad_scr):
    win_ref[...] = wt_ref[...].T.astype(BF16)
    pad_scr[...] = jnp.zeros_like(pad_scr)
    pad_scr[0:SSD_HEADS, :] = wt_dt_ref[...]
    wdt_ref[...] = pad_scr[...].T


def _cast_ssd_w_in(w_t):
    rows = 512
    main = D_INNER + SSD_CONV_DIM
    return pl.pallas_call(
        _cast_ssd_w_in_kernel, grid=(main // rows,),
        in_specs=[pl.BlockSpec((rows, D_MODEL), lambda i: (i, 0)),
                  pl.BlockSpec((SSD_HEADS, D_MODEL), lambda i: (main // SSD_HEADS, 0))],
        out_specs=(pl.BlockSpec((D_MODEL, rows), lambda i: (0, i)),
                   pl.BlockSpec((D_MODEL, LANES), lambda i: (0, 0))),
        out_shape=(jax.ShapeDtypeStruct((D_MODEL, main), BF16),
                   jax.ShapeDtypeStruct((D_MODEL, LANES), F32)),
        scratch_shapes=[pltpu.VMEM((LANES, D_MODEL), F32)],
        name="cast_ssd_w_in",
        compiler_params=pltpu.CompilerParams(dimension_semantics=("arbitrary",)),
    )(w_t, w_t)


L1_TB = 256


def _log_sigmoid(x):
    return jnp.minimum(x, 0.0) - jnp.log1p(jnp.exp(-jnp.abs(x)))


def _dot_nt(a, b):
    return lax.dot_general(a, b, (((1,), (1,)), ((), ())), preferred_element_type=F32)


def _dot_tn(a, b):
    return lax.dot_general(a, b, (((0,), (0,)), ((), ())), preferred_element_type=F32)


def _mlstm_prompt_kernel(x_ref, prew_ref, win_ref, convw_ref, convb_ref,
                         wq_ref, wk_ref, wv_ref, wg_ref, bg_ref, hnw_ref, skip_ref, wout_ref,
                         postw_ref,
                         y_ref, convst_ref, c_ref, n_ref, m_ref,
                         xs_scr, u_scr, hist_scr, xc_scr, qkv_scr, qs_scr, h_scr, hb_scr, n_scr, m_scr,
                         *xe_scr):
    tb = L1_TB
    hd_dim = MLSTM_HEAD_DIM
    t = pl.program_id(1)
    nt = pl.num_programs(1)
    wxm_ref, wz_ref, wo_ref = _mlstm_win_views(win_ref)
    q = SSD_CHUNK
    nchunk = tb // q

    @pl.when(t == 0)
    def _():
        hist_scr[...] = jnp.zeros_like(hist_scr)
        c_ref[...] = jnp.zeros_like(c_ref)
        n_scr[...] = jnp.zeros_like(n_scr)
        m_scr[...] = jnp.zeros_like(m_scr)

    x = x_ref[0]
    u = _rms(x, prew_ref[...]).astype(BF16)
    u_scr[...] = u

    nt_cols = 512
    scale = hd_dim ** -0.5
    for hd in range(MLSTM_HEADS):
        res = _dot(u, wxm_ref[:, hd * hd_dim:(hd + 1) * hd_dim])
        for c in range(nchunk):
            xe_scr[hd][c, WRAP:WRAP + q, :] = res[c * q:(c + 1) * q]
    for hd in range(MLSTM_HEADS):
        cols = slice(hd * hd_dim, (hd + 1) * hd_dim)
        _fill_wrap_rows(xe_scr[hd], hist_scr.at[hd], nchunk, q)
        for c in range(nchunk):
            xc_scr[c * q:(c + 1) * q, cols] = _silu(_conv_tile(xe_scr[hd], convw_ref, convb_ref, cols, c, q))
    for hd in range(MLSTM_HEADS):
        cols = slice(hd * hd_dim, (hd + 1) * hd_dim)
        xe_t = xe_scr[hd]
        xc_h = xc_scr[:, cols].astype(BF16)
        xm_h = jnp.concatenate([xe_t[c, WRAP:WRAP + q, :] for c in range(nchunk)], axis=0).astype(BF16)
        q_h = _dot(xc_h, wq_ref[hd])
        qkv_scr[:, cols] = q_h.astype(BF16)
        qs_scr[:, cols] = (q_h * scale).astype(BF16)
        qkv_scr[:, D_INNER + hd * hd_dim:D_INNER + (hd + 1) * hd_dim] = _dot(xc_h, wk_ref[hd]).astype(BF16)
        qkv_scr[:, 2 * D_INNER + hd * hd_dim:2 * D_INNER + (hd + 1) * hd_dim] = _dot(xm_h, wv_ref[hd]).astype(BF16)

    @pl.when(t == nt - 1)
    def _():
        _store_conv_state(convst_ref, pl.program_id(0), xe_scr, nchunk, q, hd_dim)

    gates = _dot(qkv_scr[...], wg_ref[...]) + bg_ref[...]
    li_all = gates
    lf_all = pltpu.roll(_log_sigmoid(gates), LANES - MLSTM_HEADS, axis=1)
    causal = _perm_causal(tb, q)
    bcum = _dot_exact(causal.astype(F32), lf_all)
    m_row = m_scr[0:1, :]
    inter_all = bcum + m_row
    bcum_t = bcum.T
    li_t = li_all.T
    lane_row = lax.broadcasted_iota(jnp.int32, (1, LANES), 1)
    m_new_row = m_row

    for hd in range(MLSTM_HEADS):
        cols = slice(hd * hd_dim, (hd + 1) * hd_dim)
        kcols = slice(D_INNER + hd * hd_dim, D_INNER + (hd + 1) * hd_dim)
        vcols = slice(2 * D_INNER + hd * hd_dim, 2 * D_INNER + (hd + 1) * hd_dim)
        b_col = bcum[:, hd:hd + 1]
        dmat = b_col + (li_t[hd:hd + 1, :] - bcum_t[hd:hd + 1, :])
        dmat = jnp.where(causal, dmat, -jnp.inf)
        inter = inter_all[:, hd:hd + 1]
        m_t = jnp.maximum(inter, jnp.max(dmat, axis=1, keepdims=True))
        w = jnp.exp(dmat - m_t)
        qs_h = qs_scr[:, cols]
        k_h = qkv_scr[:, kcols]
        v_h = qkv_scr[:, vcols]
        s = _dot_nt(qs_h, k_h) * w
        inter_w = jnp.exp(inter - m_t)
        c_old = c_ref[0, hd]
        n_row = n_scr[0:1, cols]
        num = _dot(s.astype(BF16), v_h) + inter_w * _dot(qs_h, c_old.astype(BF16))
        qn = jnp.sum(qs_h.astype(F32) * n_row, axis=1, keepdims=True)
        den = jnp.sum(s, axis=1, keepdims=True) + inter_w * qn
        hh = num / jnp.maximum(jnp.abs(den), jnp.exp(-m_t))
        hc = hh - jnp.mean(hh, axis=1, keepdims=True)
        h_scr[:, cols] = hc * lax.rsqrt(jnp.mean(hc * hc, axis=1, keepdims=True) + EPS)
        m_new = m_t[tb - 1:tb, :]
        b_last = b_col[tb - 1:tb, :]
        wts = jnp.exp(b_last - b_col + li_all[:, hd:hd + 1] - m_new)
        cd = jnp.exp(b_last + m_row[:, hd:hd + 1] - m_new)
        kw = k_h.astype(F32) * wts
        c_ref[0, hd] = cd * c_old + _dot_tn(kw.astype(BF16), v_h)
        n_scr[0:1, cols] = cd * n_row + jnp.sum(kw, axis=0, keepdims=True)
        m_new_row = jnp.where(lane_row == hd, m_new, m_new_row)

    m_scr[0:1, :] = m_new_row

    @pl.when(t == nt - 1)
    def _():
        n_ref[0] = n_scr[0:1, :]
        m_ref[0] = m_new_row

    for n0 in range(0, D_INNER, nt_cols):
        cols = slice(n0, n0 + nt_cols)
        o = _sigmoid(_dot(u_scr[...], wo_ref[:, cols]))
        z = _dot(u_scr[...], wz_ref[:, cols])
        hg = o * (h_scr[:, cols] * hnw_ref[:, cols]) + skip_ref[:, cols] * xc_scr[:, cols]
        hb_scr[:, cols] = (hg * _silu(z)).astype(BF16)
    out = _dot(hb_scr[...], wout_ref[...])
    _store_unpermuted(y_ref, xs_scr, x_ref[0] + _rms(out, postw_ref[...]), tb, q)


def _mlstm_prompt(x, prew, win, convw, convb, wq, wk, wv, wg, bg, hnw, skip, wout, postw):
    bsz, seq, _ = x.shape
    tb = L1_TB
    grid = (bsz, seq // tb)
    consts = (prew, win, convw, convb, wq, wk, wv, wg, bg, hnw, skip, wout, postw)
    in_specs = [pl.BlockSpec((1, tb, D_MODEL), lambda b, t: (b, t, 0))]
    in_specs += [_const_spec(c.shape) for c in consts]
    hd_dim = MLSTM_HEAD_DIM
    out_shape = (jax.ShapeDtypeStruct((bsz, seq, D_MODEL), F32),
                 jax.ShapeDtypeStruct((CONV_WIDTH - 1, bsz, D_INNER), F32),
                 jax.ShapeDtypeStruct((bsz, MLSTM_HEADS, hd_dim, hd_dim), F32),
                 jax.ShapeDtypeStruct((bsz, 1, D_INNER), F32),
                 jax.ShapeDtypeStruct((bsz, 1, LANES), F32))
    out_specs = (pl.BlockSpec((1, tb, D_MODEL), lambda b, t: (b, t, 0)),
                 pl.BlockSpec((CONV_WIDTH - 1, bsz, D_INNER), lambda b, t: (0, 0, 0)),
                 pl.BlockSpec((1, MLSTM_HEADS, hd_dim, hd_dim), lambda b, t: (b, 0, 0, 0)),
                 pl.BlockSpec((1, 1, D_INNER), lambda b, t: (b, 0, 0)),
                 pl.BlockSpec((1, 1, LANES), lambda b, t: (b, 0, 0)))
    scratch = [pltpu.VMEM((D_MODEL // LANES, tb, LANES), F32),
               pltpu.VMEM((tb, D_MODEL), BF16),
               pltpu.VMEM((MLSTM_HEADS, WRAP, hd_dim), F32),
               pltpu.VMEM((tb, D_INNER), F32),
               pltpu.VMEM((tb, 3 * D_INNER), BF16),
               pltpu.VMEM((tb, D_INNER), BF16),
               pltpu.VMEM((tb, D_INNER), F32),
               pltpu.VMEM((tb, D_INNER), BF16),
               pltpu.VMEM((SUBLANES, D_INNER), F32),
               pltpu.VMEM((SUBLANES, LANES), F32)]
    scratch += [pltpu.VMEM((tb // SSD_CHUNK, WRAP + SSD_CHUNK, hd_dim), F32) for _ in range(MLSTM_HEADS)]
    return pl.pallas_call(
        _mlstm_prompt_kernel, grid=grid, in_specs=in_specs, out_specs=out_specs,
        out_shape=out_shape, scratch_shapes=scratch, name="mlstm_prompt",
        compiler_params=pltpu.CompilerParams(
            dimension_semantics=("arbitrary", "arbitrary"), vmem_limit_bytes=VMEM_LIMIT),
    )(x, *consts)


N_SAMPLE = 128
S0_SEQS = 8


def _split3(v):
    hi = v.astype(BF16)
    r1 = v - hi.astype(F32)
    mid = r1.astype(BF16)
    lo = (r1 - mid.astype(F32)).astype(BF16)
    return hi, mid, lo


def _head_expansion_matrix(width):
    rr = lax.broadcasted_iota(jnp.int32, (3 * LANES, D_INNER), 0) & (LANES - 1)
    cc = lax.broadcasted_iota(jnp.int32, (3 * LANES, D_INNER), 1) // width
    return jnp.where(rr == cc, 1.0, 0.0).astype(BF16)


def _expand_heads(v, width):
    return _dot(jnp.concatenate(_split3(v), axis=1), _head_expansion_matrix(width))


def _ssd_sample_pre_kernel(x_ref, cs_ref, prew_ref, win_ref, wdt_ref, convw_ref, convb_ref,
                           dtb_ref, alog_ref,
                           z_ref, xact_ref, b_ref, ct_ref, xdtt_ref, dec_ref, csnew_ref):
    wz_ref, wxbc_ref = _ssd_win_views(win_ref)
    u = _rms(x_ref[...], prew_ref[...]).astype(BF16)
    xbc = _dot(u, wxbc_ref[...])
    acc = convb_ref[...] + xbc * convw_ref[CONV_WIDTH - 1:CONV_WIDTH, :]
    for k in range(CONV_WIDTH - 1):
        acc = acc + cs_ref[k] * convw_ref[k:k + 1, :]
    act = _silu(acc)
    for k in range(CONV_WIDTH - 2):
        csnew_ref[k] = cs_ref[k + 1]
    csnew_ref[CONV_WIDTH - 2] = xbc
    xact = act[:, :D_INNER]
    dt = _softplus(_dot(u, wdt_ref[...].astype(BF16)) + dtb_ref[...])
    dec_ref[...] = jnp.exp(dt * (-jnp.exp(alog_ref[...])))
    xdt = xact * _expand_heads(dt, SSD_HEAD_DIM)
    xdtt_ref[...] = xdt.T.astype(BF16)
    z_ref[...] = _dot(u, wz_ref[...])
    xact_ref[...] = xact
    b_ref[...] = act[:, D_INNER:D_INNER + SSD_GN]
    ct_ref[...] = act[:, D_INNER + SSD_GN:].T


def _ssd_sample_pre(x, cs, prew, win, wdt, convw, convb, dtb, alog):
    n = x.shape[0]
    out_shape = (jax.ShapeDtypeStruct((n, D_INNER), F32),
                 jax.ShapeDtypeStruct((n, D_INNER), F32),
                 jax.ShapeDtypeStruct((n, SSD_GN), F32),
                 jax.ShapeDtypeStruct((SSD_GN, n), F32),
                 jax.ShapeDtypeStruct((D_INNER, n), BF16),
                 jax.ShapeDtypeStruct((n, LANES), F32),
                 jax.ShapeDtypeStruct((CONV_WIDTH - 1, n, SSD_CONV_DIM), F32))
    return pl.pallas_call(
        _ssd_sample_pre_kernel, out_shape=out_shape, name="ssd_sample_pre",
        compiler_params=pltpu.CompilerParams(vmem_limit_bytes=VMEM_LIMIT),
    )(x, cs, prew, win, wdt, convw, convb, dtb, alog)


def _ssd_sample_state_kernel(dec_ref, h_ref, xdtt_ref, b_ref, ct_ref, hnew_ref, yt_ref):
    j = pl.program_id(0)
    gw = D_INNER // SSD_GROUPS
    hpg = SSD_HEADS // SSD_GROUPS

    @pl.when(j == 0)
    def _():
        yt_ref[...] = jnp.zeros_like(yt_ref)

    rowid = lax.broadcasted_iota(jnp.int32, (N_SAMPLE, SSD_STATE), 0)
    colid = lax.broadcasted_iota(jnp.int32, (SSD_STATE, N_SAMPLE), 1)
    for pp in range(S0_SEQS // 2):
        seqs = (j * S0_SEQS + 2 * pp, j * S0_SEQS + 2 * pp + 1)
        for g in range(SSD_GROUPS):
            bg = b_ref[:, g * SSD_STATE:(g + 1) * SSD_STATE]
            ctg = ct_ref[g * SSD_STATE:(g + 1) * SSD_STATE, :]
            rb = jnp.concatenate([jnp.where(rowid == s, bg, 0.0) for s in seqs], axis=1).astype(BF16)
            upd = _dot(xdtt_ref[g * gw:(g + 1) * gw, :], rb)
            for i, s in enumerate(seqs):
                for r in range(hpg):
                    rows = slice(g * gw + r * SSD_HEAD_DIM, g * gw + (r + 1) * SSD_HEAD_DIM)
                    dec = dec_ref[s * SSD_HEADS + g * hpg + r]
                    hn = (h_ref[2 * pp + i, rows, :] * dec
                          + upd[r * SSD_HEAD_DIM:(r + 1) * SSD_HEAD_DIM, i * SSD_STATE:(i + 1) * SSD_STATE])
                    hnew_ref[2 * pp + i, rows, :] = hn
            hn_pair = jnp.concatenate(
                [hnew_ref[2 * pp + i, g * gw:(g + 1) * gw, :] for i in range(2)], axis=1).astype(BF16)
            wc = jnp.concatenate([jnp.where(colid == s, ctg, 0.0) for s in seqs], axis=0).astype(BF16)
            yt_ref[g * gw:(g + 1) * gw, :] += _dot(hn_pair, wc)


def _ssd_sample_state(dec_flat, h, xdtt, b, ct):
    n = h.shape[0]
    grid = (n // S0_SEQS,)
    in_specs = [pl.BlockSpec(memory_space=pltpu.SMEM),
                pl.BlockSpec((S0_SEQS, D_INNER, SSD_STATE), lambda j: (j, 0, 0)),
                _const_spec(xdtt.shape), _const_spec(b.shape), _const_spec(ct.shape)]
    out_shape = (jax.ShapeDtypeStruct(h.shape, F32),
                 jax.ShapeDtypeStruct((D_INNER, n), F32))
    out_specs = (pl.BlockSpec((S0_SEQS, D_INNER, SSD_STATE), lambda j: (j, 0, 0)),
                 pl.BlockSpec((D_INNER, n), lambda j: (0, 0)))
    return pl.pallas_call(
        _ssd_sample_state_kernel, grid=grid, in_specs=in_specs, out_specs=out_specs,
        out_shape=out_shape, name="ssd_sample_state",
        compiler_params=pltpu.CompilerParams(
            dimension_semantics=("arbitrary",), vmem_limit_bytes=VMEM_LIMIT),
    )(dec_flat, h, xdtt, b, ct)


def _head_cols(a, hd):
    return a[:, hd * MLSTM_HEAD_DIM:(hd + 1) * MLSTM_HEAD_DIM]


def _sample_mid_kernel(x_ref, yt_ref, xact_ref, z_ref, dexp_ref, normw_ref, wout0_ref, postw0_ref,
                       mcs_ref, n0_ref, m0_ref, prew1_ref, win1_ref, convw_ref,
                       convb_ref, wq_ref, wk_ref, wv_ref, wg_ref, bg_ref,
                       x1_ref, z1_ref, o1_ref, xc_ref, v_ref, kt_ref, wv_out_ref, qs_ref,
                       s_ref, iw_ref, hden_ref, mcsnew_ref, nnew_ref, mnew_ref):
    gw = D_INNER // SSD_GROUPS
    wxm_ref, wz1_ref, wo1_ref = _mlstm_win_views(win1_ref)
    y = yt_ref[...].T + dexp_ref[...] * xact_ref[...]
    y = y * _silu(z_ref[...])
    parts = []
    for g in range(SSD_GROUPS):
        cols = slice(g * gw, (g + 1) * gw)
        parts.append(_rms(y[:, cols], normw_ref[:, cols]).astype(BF16))
    out0 = _dot(jnp.concatenate(parts, axis=1), wout0_ref[...])
    x1 = x_ref[...] + _rms(out0, postw0_ref[...])
    x1_ref[...] = x1

    u = _rms(x1, prew1_ref[...]).astype(BF16)
    xm = _dot(u, wxm_ref[...])
    z1_ref[...] = _dot(u, wz1_ref[...])
    o1_ref[...] = _dot(u, wo1_ref[...])
    acc = convb_ref[...] + xm * convw_ref[CONV_WIDTH - 1:CONV_WIDTH, :]
    for k in range(CONV_WIDTH - 1):
        acc = acc + mcs_ref[k] * convw_ref[k:k + 1, :]
    xc = _silu(acc)
    xc_ref[...] = xc
    for k in range(CONV_WIDTH - 2):
        mcsnew_ref[k] = mcs_ref[k + 1]
    mcsnew_ref[CONV_WIDTH - 2] = xm

    scale = MLSTM_HEAD_DIM ** -0.5
    qs, ks, vs = [], [], []
    for hd in range(MLSTM_HEADS):
        xc_h = _head_cols(xc, hd).astype(BF16)
        xm_h = _head_cols(xm, hd).astype(BF16)
        qs.append(_dot(xc_h, wq_ref[hd]))
        ks.append(_dot(xc_h, wk_ref[hd]))
        vs.append(_dot(xm_h, wv_ref[hd]))
    qkv = jnp.concatenate(qs + ks + vs, axis=1).astype(BF16)
    gates = _dot(qkv, wg_ref[...]) + bg_ref[...]
    li = gates
    lf = pltpu.roll(_log_sigmoid(gates), LANES - MLSTM_HEADS, axis=1)
    m0 = m0_ref[...]
    inter = lf + m0
    m_t = jnp.maximum(inter, li)
    w = jnp.exp(li - m_t)
    iw = jnp.exp(inter - m_t)
    mnew_ref[...] = m_t

    lane = lax.broadcasted_iota(jnp.int32, (N_SAMPLE, LANES), 1)
    qk_arr = jnp.zeros((N_SAMPLE, LANES), F32)
    qn_arr = jnp.zeros((N_SAMPLE, LANES), F32)
    kb_parts = []
    for hd in range(MLSTM_HEADS):
        cols = slice(hd * MLSTM_HEAD_DIM, (hd + 1) * MLSTM_HEAD_DIM)
        q_h = (qs[hd] * scale).astype(BF16).astype(F32)
        k_h = ks[hd].astype(BF16).astype(F32)
        n_h = n0_ref[hd]
        qk_arr = jnp.where(lane == hd, jnp.sum(q_h * k_h, axis=1, keepdims=True), qk_arr)
        qn_arr = jnp.where(lane == hd, jnp.sum(q_h * n_h, axis=1, keepdims=True), qn_arr)
        w_h = w[:, hd:hd + 1]
        iw_h = iw[:, hd:hd + 1]
        nnew_ref[hd] = iw_h * n_h + w_h * k_h
        wv_out_ref[:, cols] = w_h * vs[hd]
        v_ref[:, cols] = vs[hd]
        qs_ref[:, cols] = q_h
        kb_parts.append(k_h)
    kt_ref[...] = jnp.concatenate(kb_parts, axis=1).T.astype(BF16)
    s = qk_arr * w
    den = s + iw * qn_arr
    s_ref[...] = s
    iw_ref[...] = iw
    hden_ref[...] = jnp.maximum(jnp.abs(den), jnp.exp(-m_t))


def _sample_mid(*args):
    n = N_SAMPLE
    wide = jax.ShapeDtypeStruct((n, D_INNER), F32)
    small = jax.ShapeDtypeStruct((n, LANES), F32)
    out_shape = (jax.ShapeDtypeStruct((n, D_MODEL), F32),
                 wide, wide, wide, wide,
                 jax.ShapeDtypeStruct((D_INNER, n), BF16),
                 wide, wide,
                 small, small, small,
                 jax.ShapeDtypeStruct((CONV_WIDTH - 1, n, D_INNER), F32),
                 jax.ShapeDtypeStruct((MLSTM_HEADS, n, MLSTM_HEAD_DIM), F32), small)
    return pl.pallas_call(
        _sample_mid_kernel, out_shape=out_shape, name="sample_mid",
        compiler_params=pltpu.CompilerParams(vmem_limit_bytes=VMEM_LIMIT),
    )(*args)


def _mlstm_sample_post_kernel(x1_ref, num_ref, v_ref, s_ref, iw_ref, hden_ref, o1_ref, z1_ref,
                              xc_ref, hnw_ref, skip_ref, wout_ref, postw_ref, y_ref):
    parts = []
    for hd in range(MLSTM_HEADS):
        cols = slice(hd * MLSTM_HEAD_DIM, (hd + 1) * MLSTM_HEAD_DIM)
        num = s_ref[:, hd:hd + 1] * v_ref[:, cols] + iw_ref[:, hd:hd + 1] * num_ref[:, cols]
        hh = num / hden_ref[:, hd:hd + 1]
        hc = hh - jnp.mean(hh, axis=1, keepdims=True)
        hn = hc * lax.rsqrt(jnp.mean(hc * hc, axis=1, keepdims=True) + EPS)
        hg = _sigmoid(o1_ref[:, cols]) * (hn * hnw_ref[:, cols]) + skip_ref[:, cols] * xc_ref[:, cols]
        parts.append((hg * _silu(z1_ref[:, cols])).astype(BF16))
    out = _dot(jnp.concatenate(parts, axis=1), wout_ref[...])
    y_ref[...] = x1_ref[...] + _rms(out, postw_ref[...])


def _mlstm_sample_post(*args):
    return pl.pallas_call(
        _mlstm_sample_post_kernel, out_shape=jax.ShapeDtypeStruct((N_SAMPLE, D_MODEL), F32),
        name="mlstm_sample_post",
        compiler_params=pltpu.CompilerParams(vmem_limit_bytes=VMEM_LIMIT),
    )(*args)


def kernel(x_prompt, x_sample, state_ssd_conv, state_ssd, state_mlstm_conv, state_mlstm_c, state_mlstm_n, state_mlstm_m, pre_norm_w, post_norm_w, ssd_w_in, ssd_conv_w, ssd_conv_b, ssd_dt_bias, ssd_a_log, ssd_d, ssd_norm_w, ssd_w_out, ml_w_in, ml_conv_w, ml_conv_b, ml_w_q, ml_w_k, ml_w_v, ml_w_gate, ml_b_gate, ml_head_norm_w, ml_skip, ml_w_out):
    win0, wdt = _cast_ssd_w_in(jnp.transpose(ssd_w_in[0]))
    dtb = _pad_lanes(ssd_dt_bias[0][None])
    alog = _pad_lanes(ssd_a_log[0][None])
    dexp = jnp.repeat(ssd_d[0], SSD_HEAD_DIM)[None]
    prew0, postw0 = pre_norm_w[0][None], post_norm_w[0][None]
    prew1, postw1 = pre_norm_w[1][None], post_norm_w[1][None]
    convw0, convb0 = ssd_conv_w[0], ssd_conv_b[0][None]
    normw0 = ssd_norm_w[0][None]
    wout0 = ssd_w_out[0].astype(BF16)
    win1 = ml_w_in[0].astype(BF16)
    wg = _pad_lanes(ml_w_gate[0]).astype(BF16)
    bg = _pad_lanes(ml_b_gate[0][None])
    convw1, convb1 = ml_conv_w[0], ml_conv_b[0][None]
    wq, wk, wv = ml_w_q[0].astype(BF16), ml_w_k[0].astype(BF16), ml_w_v[0].astype(BF16)
    hnw, skip = ml_head_norm_w[0][None], ml_skip[0][None]
    wout1 = ml_w_out[0].astype(BF16)

    n = x_sample.shape[0]
    xs = x_sample.reshape(n, D_MODEL)
    cs0 = jnp.transpose(state_ssd_conv[0], (1, 0, 2))
    z0, xact0, b0, ct0, xdtt0, dec0, csnew0 = _ssd_sample_pre(
        xs, cs0, prew0, win0, wdt, convw0, convb0, dtb, alog)
    h0 = state_ssd[0].reshape(n, D_INNER, SSD_STATE)
    hnew0, yt0 = _ssd_sample_state(dec0[:, :SSD_HEADS].reshape(-1), h0, xdtt0, b0, ct0)
    mcs = jnp.transpose(state_mlstm_conv[0], (1, 0, 2))
    n0 = jnp.transpose(state_mlstm_n[0], (1, 0, 2))
    m0 = _pad_lanes(state_mlstm_m[0])
    (x1, z1, o1, xc1, v1, kt1, wv1, qs1, s1, iw1, hden1, mcsnew, nnew, mnew) = _sample_mid(
        xs, yt0, xact0, z0, dexp, normw0, wout0, postw0,
        mcs, n0, m0, prew1, win1, convw1, convb1, wq, wk, wv, wg, bg)

    bp = x_prompt.shape[0]
    y0, p_sc, p_sh, cnew, num1 = _ssd_prompt(
        x_prompt, prew0, win0, wdt, convw0, convb0, dtb, alog, dexp, normw0, wout0, postw0,
        _head_expansion_matrix(SSD_HEAD_DIM),
        iw1[:, :MLSTM_HEADS].reshape(-1), state_mlstm_c[0], kt1, wv1, qs1)
    y_prompt, p_mc, p_c, p_n, p_m = _mlstm_prompt(
        y0, prew1, win1, convw1, convb1, wq, wk, wv, wg, bg, hnw, skip, wout1, postw1)
    p_sc = jnp.transpose(p_sc, (1, 0, 2))[None]
    p_sh = p_sh.reshape(1, bp, SSD_HEADS, SSD_HEAD_DIM, SSD_STATE)
    p_mc = jnp.transpose(p_mc, (1, 0, 2))[None]
    p_c = p_c[None]
    p_n = p_n.reshape(1, bp, MLSTM_HEADS, MLSTM_HEAD_DIM)
    p_m = p_m[:, 0, :MLSTM_HEADS][None]

    y_s = _mlstm_sample_post(x1, num1, v1, s1, iw1, hden1, o1, z1, xc1, hnw, skip, wout1, postw1)

    y_sample = y_s.reshape(n, 1, D_MODEL)
    s_sc = jnp.transpose(csnew0, (1, 0, 2))[None]
    s_sh = hnew0.reshape(1, n, SSD_HEADS, SSD_HEAD_DIM, SSD_STATE)
    s_mc = jnp.transpose(mcsnew, (1, 0, 2))[None]
    s_c = cnew[None]
    s_n = jnp.transpose(nnew, (1, 0, 2))[None]
    s_m = mnew[:, :MLSTM_HEADS][None]
    return (y_prompt, y_sample, p_sc, s_sc, p_sh, s_sh, p_mc, s_mc, p_c, s_c, p_n, s_n, p_m, s_m)
```
